```python
import jax, jax.numpy as jnp
from jax import lax
import numpy as np

D_MODEL = 1024
BATCH = 2
SEQ = 16384
DEPTH = 4

SC_WIDTH = D_MODEL
CONV_K = 3
GMLP_WIDTH = D_MODEL
GMLP_GROUPS = 8
GMLP_GROUP_DIM = GMLP_WIDTH // GMLP_GROUPS
CHUNK = 128
N_EXPERTS = 32
TOP_K = 4
D_FF = D_MODEL
SWIGLU_LIMIT = 7.0
SWIGLU_ALPHA = 1.702
MOE_BLOCK = 256
NORM_EPS = 1e-6
D_IN = 3 * SC_WIDTH + 2 * GMLP_WIDTH + 2 * D_MODEL

kernel_name = "hybrid_conv_gmlp_moe_adaln_trunk"


def rms_norm(x, g):
    xf = x.astype(jnp.float32)
    y = xf * lax.rsqrt(jnp.mean(xf * xf, axis=-1, keepdims=True) + NORM_EPS)
    return (y * g.astype(jnp.float32)).astype(x.dtype)


def layer_norm(x, g):
    xf = x.astype(jnp.float32)
    mu = jnp.mean(xf, axis=-1, keepdims=True)
    xc = xf - mu
    var = jnp.mean(xc * xc, axis=-1, keepdims=True)
    return (xc * lax.rsqrt(var + NORM_EPS) * g.astype(jnp.float32)).astype(x.dtype)


def modulate(h, shift, scale):
    return h * (1 + scale) + shift


def causal_dwconv3(z, w):
    z1 = jnp.pad(z, ((0, 0), (1, 0), (0, 0)))[:, :-1]
    z2 = jnp.pad(z, ((0, 0), (2, 0), (0, 0)))[:, :-2]
    return w[0] * z2 + w[1] * z1 + w[2] * z


def chunked_spatial_gate(u, v, w_s, b_s):
    bsz, seq, _ = v.shape
    vc = v.reshape(bsz, seq // CHUNK, CHUNK, GMLP_GROUPS, GMLP_GROUP_DIM)
    mask = jnp.tril(jnp.ones((CHUNK, CHUNK), dtype=w_s.dtype))
    s = jnp.einsum("gts,bnsgc->bntgc", w_s * mask, vc) + b_s.T[:, :, None]
    return u * s.reshape(bsz, seq, GMLP_WIDTH)


def hybrid_mixer(h, w_in, conv_w, w_s, b_s, ln_v_g, w_out):
    z = h @ w_in
    o1 = SC_WIDTH
    o2 = 2 * SC_WIDTH
    o3 = 3 * SC_WIDTH
    o4 = o3 + GMLP_WIDTH
    o5 = o4 + GMLP_WIDTH
    o6 = o5 + D_MODEL
    sc_x, sc_b, sc_c = z[..., :o1], z[..., o1:o2], z[..., o2:o3]
    u, v = z[..., o3:o4], z[..., o4:o5]
    gate_a, gate_b = z[..., o5:o6], z[..., o6:]
    y_a = sc_b * causal_dwconv3(sc_c * sc_x, conv_w)
    u = jax.nn.gelu(u, approximate=False)
    v = layer_norm(jax.nn.gelu(v, approximate=False), ln_v_g)
    y_b = chunked_spatial_gate(u, v, w_s, b_s)
    m = jax.nn.sigmoid(gate_a) * y_a + jax.nn.sigmoid(gate_b) * y_b
    return m @ w_out


def moe_ffn(h, router_w, router_b, w_gu, b_gu, w_dn, b_dn):
    n_tok, d = h.shape
    logits = (h @ router_w + router_b).astype(jnp.float32)
    top_logit, top_e = lax.top_k(logits, TOP_K)
    top_w = jax.nn.softmax(top_logit, axis=-1)
    n_assign = n_tok * TOP_K
    flat_e = top_e.reshape(-1)
    order = jnp.argsort(flat_e)
    sorted_e = flat_e[order]
    sorted_tok = order // TOP_K
    sorted_w = top_w.reshape(-1)[order].astype(h.dtype)
    counts = jnp.bincount(flat_e, length=N_EXPERTS)
    padded = (counts + MOE_BLOCK - 1) // MOE_BLOCK * MOE_BLOCK
    pad_end = jnp.cumsum(padded)
    pad_start = pad_end - padded
    grp_start = jnp.cumsum(counts) - counts
    dest = pad_start[sorted_e] + (jnp.arange(n_assign) - grp_start[sorted_e])
    n_blocks = -(-n_assign // MOE_BLOCK) + N_EXPERTS
    row_tok = jnp.zeros((n_blocks * MOE_BLOCK,), jnp.int32).at[dest].set(sorted_tok)
    block_e = jnp.minimum(
        jnp.searchsorted(pad_end, jnp.arange(n_blocks) * MOE_BLOCK, side="right"), N_EXPERTS - 1)
    xb = h[row_tok].reshape(n_blocks, MOE_BLOCK, d)

    def expert_block(args):
        xs, e = args
        gu = xs @ w_gu[e] + b_gu[e]
        gate, up = gu[:, :D_FF], gu[:, D_FF:]
        gate = jnp.minimum(gate, SWIGLU_LIMIT)
        up = jnp.clip(up, -SWIGLU_LIMIT, SWIGLU_LIMIT)
        act = gate * jax.nn.sigmoid(SWIGLU_ALPHA * gate) * (up + 1)
        return act @ w_dn[e] + b_dn[e]

    yb = lax.map(expert_block, (xb, block_e)).reshape(-1, d)
    y_sorted = yb[dest] * sorted_w[:, None]
    return jax.ops.segment_sum(y_sorted, sorted_tok, num_segments=n_tok)


def setup_inputs(seed: int = 0) -> dict:
    key = jax.random.key(seed)
    ks = jax.random.split(key, 19)

    def nrm(k, shape, std):
        return jax.random.normal(k, shape, jnp.float32) * std

    return {
        "x": nrm(ks[0], (BATCH, SEQ, D_MODEL), 1.0),
        "c": nrm(ks[1], (BATCH, D_MODEL), 1.0),
        "w_ada": nrm(ks[2], (DEPTH, D_MODEL, 6 * D_MODEL), 0.5 * D_MODEL ** -0.5),
        "b_ada": nrm(ks[3], (DEPTH, 6 * D_MODEL), 0.1),
        "norm_mix_g": 1.0 + nrm(ks[4], (DEPTH, D_MODEL), 0.05),
        "w_in": nrm(ks[5], (DEPTH, D_MODEL, D_IN), D_MODEL ** -0.5),
        "conv_w": nrm(ks[6], (DEPTH, CONV_K, SC_WIDTH), CONV_K ** -0.5),
        "w_spatial": nrm(ks[7], (DEPTH, GMLP_GROUPS, CHUNK, CHUNK), CHUNK ** -0.5),
        "b_spatial": 1.0 + nrm(ks[8], (DEPTH, GMLP_GROUPS, CHUNK), 0.1),
        "ln_v_g": 1.0 + nrm(ks[9], (DEPTH, GMLP_WIDTH), 0.05),
        "w_out": nrm(ks[10], (DEPTH, D_MODEL, D_MODEL), D_MODEL ** -0.5),
        "norm_ffn_g": 1.0 + nrm(ks[11], (DEPTH, D_MODEL), 0.05),
        "router_w": nrm(ks[12], (DEPTH, D_MODEL, N_EXPERTS), D_MODEL ** -0.5),
        "router_b": nrm(ks[13], (DEPTH, N_EXPERTS), 0.01),
        "w_gate_up": nrm(ks[14], (DEPTH, N_EXPERTS, D_MODEL, 2 * D_FF), D_MODEL ** -0.5),
        "b_gate_up": nrm(ks[15], (DEPTH, N_EXPERTS, 2 * D_FF), 0.02),
        "w_down": nrm(ks[16], (DEPTH, N_EXPERTS, D_FF, D_MODEL), D_FF ** -0.5),
        "b_down": nrm(ks[17], (DEPTH, N_EXPERTS, D_MODEL), 0.02),
        "final_norm_g": 1.0 + nrm(ks[18], (D_MODEL,), 0.05),
    }


def reference(x, c, w_ada, b_ada, norm_mix_g, w_in, conv_w, w_spatial, b_spatial, ln_v_g,
              w_out, norm_ffn_g, router_w, router_b, w_gate_up, b_gate_up, w_down, b_down,
              final_norm_g):
    bsz, seq, d = x.shape
    cond = jax.nn.silu(c)
    for l in range(DEPTH):
        mod = (cond @ w_ada[l] + b_ada[l])[:, None, :]
        sh_m, sc_m, g_m, sh_f, sc_f, g_f = jnp.split(mod, 6, axis=-1)
        h = modulate(rms_norm(x, norm_mix_g[l]), sh_m, sc_m)
        x = x + g_m * hybrid_mixer(h, w_in[l], conv_w[l], w_spatial[l], b_spatial[l],
                                   ln_v_g[l], w_out[l])
        h = modulate(rms_norm(x, norm_ffn_g[l]), sh_f, sc_f)
        y = moe_ffn(h.reshape(-1, d), router_w[l], router_b[l], w_gate_up[l], b_gate_up[l],
                    w_down[l], b_down[l])
        x = x + g_f * y.reshape(bsz, seq, d)
    return rms_norm(x, final_norm_g)
```

```python
import functools
import math

import jax
import jax.numpy as jnp
from jax import lax
from jax.experimental import pallas as pl
from jax.experimental.pallas import tpu as pltpu

CONV_K = 3
GROUPS = 8
CHUNK = 128
TOP_K = 4
SWIGLU_LIMIT = 7.0
SWIGLU_ALPHA = 1.702
NORM_EPS = 1e-6

V7X_SUBLANES = 8
V7X_LANES = 128
V7X_VMEM_LIMIT_BYTES = 56 * 1024 * 1024

MIX_TILE = 512
ROW_CHUNK = CHUNK
DISPATCH_TILE = 512
COMBINE_TILE = 256
EXPERT_BLOCK = 512


def _split_bf16(a):
    hi = a.astype(jnp.bfloat16)
    lo = (a - hi.astype(jnp.float32)).astype(jnp.bfloat16)
    return hi, lo


def _gelu(a):
    return 0.5 * a * (1.0 + lax.erf(a * (1.0 / math.sqrt(2.0))))


def _sigmoid(a):
    return 1.0 / (1.0 + jnp.exp(-a))


def _mod_kernel(c_ref, w_ref, b_ref, o_ref):
    c = c_ref[...]
    s = c * _sigmoid(c)
    s_hi, s_lo = _split_bf16(s)
    w_hi, w_lo = _split_bf16(w_ref[...])
    dot = functools.partial(jnp.dot, preferred_element_type=jnp.float32)
    o_ref[...] = dot(s_hi, w_hi) + dot(s_lo, w_hi) + dot(s_hi, w_lo) + b_ref[...]


def _modulation(c, w_ada, b_ada):
    depth, d, d6 = w_ada.shape
    bsz = c.shape[0]
    rows = -(-bsz // V7X_SUBLANES) * V7X_SUBLANES
    c_pad = jnp.zeros((rows, d), jnp.float32).at[:bsz].set(c)
    n_col = d6 // d
    out = pl.pallas_call(
        _mod_kernel,
        grid=(depth, n_col),
        in_specs=[
            pl.BlockSpec((rows, d), lambda l, j: (0, 0)),
            pl.BlockSpec((None, d, d), lambda l, j: (l, 0, j)),
            pl.BlockSpec((None, 1, d), lambda l, j: (l, 0, j)),
        ],
        out_specs=pl.BlockSpec((None, rows, d), lambda l, j: (l, 0, j)),
        out_shape=jax.ShapeDtypeStruct((depth, rows, d6), jnp.float32),
    )(c_pad, w_ada, b_ada.reshape(depth, 1, d6))
    return out[:, :bsz].reshape(depth, bsz, 1, d6)


def _mixer_kernel(x_ref, mod_ref, gmix_ref, win_ref, conv_ref, wsp_ref, bsp_ref, lng_ref,
                  wout_ref, gffn_ref, rwt_ref, rb_ref,
                  x1_ref, h2_ref, te_ref, tw_ref, rk_ref, cnt_ref,
                  hbuf, zbuf, halo_x, halo_c, mabuf, mbuf, run_ref):
    tm, d = x_ref.shape
    n_exp = rwt_ref.shape[0]
    n_chunks = tm // ROW_CHUNK
    first_tile = jnp.logical_and(pl.program_id(0) == 0, pl.program_id(1) == 0)
    dot = functools.partial(jnp.dot, preferred_element_type=jnp.float32)

    @pl.when(pl.program_id(1) == 0)
    def _():
        halo_x[...] = jnp.zeros_like(halo_x)
        halo_c[...] = jnp.zeros_like(halo_c)

    @pl.when(first_tile)
    def _():
        run_ref[...] = jnp.zeros_like(run_ref)

    mod = mod_ref[...]
    sh_m, sc_m, g_m = mod[:, 0:d], mod[:, d:2 * d], mod[:, 2 * d:3 * d]
    sh_f, sc_f, g_f = mod[:, 3 * d:4 * d], mod[:, 4 * d:5 * d], mod[:, 5 * d:6 * d]

    x = x_ref[...]
    gs = gmix_ref[...] * (1.0 + sc_m)
    r = lax.rsqrt(jnp.mean(x * x, axis=-1, keepdims=True) + NORM_EPS)
    hbuf[...] = (x * r * gs + sh_m).astype(jnp.bfloat16)

    h = hbuf[...]
    zbuf[:, 0:3 * d] = dot(h, win_ref[:, 0:3 * d])
    zbuf[:, 3 * d:4 * d] = dot(h, win_ref[:, 5 * d:6 * d])

    cw = conv_ref[...]
    w0, w1, w2 = cw[0:1, :], cw[1:2, :], cw[2:3, :]
    row8 = lax.broadcasted_iota(jnp.int32, (V7X_SUBLANES, d), 0)

    def branch_a(i, carry):
        r0 = pl.multiple_of(i * ROW_CHUNK, ROW_CHUNK)
        rows = pl.ds(r0, ROW_CHUNK)
        p0 = zbuf[rows, 0:d] * zbuf[rows, 2 * d:3 * d]
        rp = pl.ds(pl.multiple_of(jnp.maximum(r0 - V7X_SUBLANES, 0), V7X_SUBLANES), V7X_SUBLANES)
        prev_in = zbuf[rp, 0:d] * zbuf[rp, 2 * d:3 * d]
        prev = jnp.where(i == 0, halo_x[...] * halo_c[...], prev_in)
        r1 = pltpu.roll(p0, 1, 0)
        r2 = pltpu.roll(p0, 2, 0)
        q1 = pltpu.roll(prev, 1, 0)
        q2 = pltpu.roll(prev, 2, 0)
        p1 = jnp.concatenate([jnp.where(row8 < 1, q1, r1[0:V7X_SUBLANES]), r1[V7X_SUBLANES:]], axis=0)
        p2 = jnp.concatenate([jnp.where(row8 < 2, q2, r2[0:V7X_SUBLANES]), r2[V7X_SUBLANES:]], axis=0)
        y_a = zbuf[rows, d:2 * d] * (w0 * p2 + w1 * p1 + w2 * p0)
        mabuf[rows, :] = _sigmoid(zbuf[rows, 3 * d:4 * d]) * y_a
        return carry

    lax.fori_loop(0, n_chunks, branch_a, 0)
    last8 = pl.ds(tm - V7X_SUBLANES, V7X_SUBLANES)
    halo_x[...] = zbuf[last8, 0:d]
    halo_c[...] = zbuf[last8, 2 * d:3 * d]

    zbuf[:, 0:2 * d] = dot(h, win_ref[:, 3 * d:5 * d])
    zbuf[:, 2 * d:3 * d] = dot(h, win_ref[:, 6 * d:7 * d])

    tri = (lax.broadcasted_iota(jnp.int32, (CHUNK, CHUNK), 1)
           <= lax.broadcasted_iota(jnp.int32, (CHUNK, CHUNK), 0))
    wm = [jnp.where(tri, wsp_ref[g], 0.0).astype(jnp.bfloat16) for g in range(GROUPS)]
    gd = d // GROUPS
    lng = lng_ref[...]

    def branch_b(i, carry):
        r0 = pl.multiple_of(i * ROW_CHUNK, ROW_CHUNK)
        rows = pl.ds(r0, ROW_CHUNK)
        gv = _gelu(zbuf[rows, d:2 * d])
        mu = jnp.mean(gv, axis=-1, keepdims=True)
        vc = gv - mu
        var = jnp.mean(vc * vc, axis=-1, keepdims=True)
        vn = (vc * lax.rsqrt(var + NORM_EPS) * lng).astype(jnp.bfloat16)
        s = jnp.concatenate(
            [dot(wm[g], vn[:, g * gd:(g + 1) * gd]) for g in range(GROUPS)], axis=1)
        y_b = _gelu(zbuf[rows, 0:d]) * (s + bsp_ref[...])
        m = mabuf[rows, :] + _sigmoid(zbuf[rows, 2 * d:3 * d]) * y_b
        mbuf[rows, :] = m.astype(jnp.bfloat16)
        return carry

    lax.fori_loop(0, n_chunks, branch_b, 0)

    x1 = x_ref[...] + g_m * dot(mbuf[...], wout_ref[...])
    x1_ref[...] = x1
    gs2 = gffn_ref[...] * (1.0 + sc_f)
    r2n = lax.rsqrt(jnp.mean(x1 * x1, axis=-1, keepdims=True) + NORM_EPS)
    h2 = x1 * r2n * gs2 + sh_f
    h2_ref[...] = h2

    nt = (((1,), (1,)), ((), ()))
    dg = functools.partial(lax.dot_general, dimension_numbers=nt,
                           preferred_element_type=jnp.float32)
    h_hi, h_lo = _split_bf16(h2)
    r_hi, r_lo = _split_bf16(rwt_ref[...])
    logits = dg(r_hi, h_hi) + dg(r_hi, h_lo) + dg(r_lo, h_hi) + rb_ref[...]

    iota_e = lax.broadcasted_iota(jnp.int32, (n_exp, tm), 0)
    vals = logits
    tops, idxs, hots = [], [], []
    for _ in range(TOP_K):
        mx = jnp.max(vals, axis=0, keepdims=True)
        ix = jnp.min(jnp.where(vals == mx, iota_e, n_exp), axis=0, keepdims=True)
        hot = iota_e == ix
        vals = jnp.where(hot, -jnp.inf, vals)
        tops.append(mx)
        idxs.append(ix)
        hots.append(hot)
    exps = [jnp.exp(t - tops[0]) for t in tops]
    denom = exps[0] + exps[1] + exps[2] + exps[3]
    for k in range(TOP_K):
        tw_ref[k:k + 1, :] = exps[k] / denom
        te_ref[k:k + 1, :] = idxs[k]

    member = jnp.where(hots[0] | hots[1] | hots[2] | hots[3], 1.0, 0.0)
    upper = (lax.broadcasted_iota(jnp.int32, (tm, tm), 0)
             < lax.broadcasted_iota(jnp.int32, (tm, tm), 1))
    before = dot(member.astype(jnp.bfloat16), jnp.where(upper, 1.0, 0.0).astype(jnp.bfloat16))
    run = run_ref[...]
    pos = before + run[:, 0:1]
    for k in range(TOP_K):
        rk = jnp.sum(jnp.where(hots[k], pos, 0.0), axis=0, keepdims=True)
        rk_ref[k:k + 1, :] = rk.astype(jnp.int32)
    run = run + jnp.sum(member, axis=1, keepdims=True)
    run_ref[...] = run
    cnt_ref[...] = run.astype(jnp.int32)


def _mixer(x, mod_l, gmix, win_bf, conv_w, wsp, bsp_full, lng, wout_bf, gffn, rwt, rb):
    bsz, seq, d = x.shape
    n_exp = rwt.shape[0]
    tm = MIX_TILE
    n_tok = bsz * seq
    tiles = seq // tm
    const = pl.Buffered(1)

    def cspec(shape):
        return pl.BlockSpec(shape, lambda b, s: (0,) * len(shape), pipeline_mode=const)

    out_shape = (
        jax.ShapeDtypeStruct((bsz, seq, d), jnp.float32),
        jax.ShapeDtypeStruct((n_tok, d), jnp.float32),
        jax.ShapeDtypeStruct((TOP_K, n_tok), jnp.int32),
        jax.ShapeDtypeStruct((TOP_K, n_tok), jnp.float32),
        jax.ShapeDtypeStruct((TOP_K, n_tok), jnp.int32),
        jax.ShapeDtypeStruct((n_exp, V7X_LANES), jnp.int32),
    )
    tok = lambda b, s: (b * tiles + s, 0)
    tok_t = lambda b, s: (0, b * tiles + s)
    return pl.pallas_call(
        _mixer_kernel,
        grid=(bsz, tiles),
        in_specs=[
            pl.BlockSpec((None, tm, d), lambda b, s: (b, s, 0)),
            pl.BlockSpec((None, 1, 6 * d), lambda b, s: (b, 0, 0)),
            cspec((1, d)),
            cspec(win_bf.shape),
            cspec(conv_w.shape),
            cspec(wsp.shape),
            cspec(bsp_full.shape),
            cspec((1, d)),
            cspec(wout_bf.shape),
            cspec((1, d)),
            cspec(rwt.shape),
            cspec(rb.shape),
        ],
        out_specs=(
            pl.BlockSpec((None, tm, d), lambda b, s: (b, s, 0)),
            pl.BlockSpec((tm, d), tok),
            pl.BlockSpec((TOP_K, tm), tok_t),
            pl.BlockSpec((TOP_K, tm), tok_t),
            pl.BlockSpec((TOP_K, tm), tok_t),
            pl.BlockSpec((n_exp, V7X_LANES), lambda b, s: (0, 0)),
        ),
        out_shape=out_shape,
        scratch_shapes=[
            pltpu.VMEM((tm, d), jnp.bfloat16),
            pltpu.VMEM((tm, 4 * d), jnp.float32),
            pltpu.VMEM((V7X_SUBLANES, d), jnp.float32),
            pltpu.VMEM((V7X_SUBLANES, d), jnp.float32),
            pltpu.VMEM((tm, d), jnp.float32),
            pltpu.VMEM((tm, d), jnp.bfloat16),
            pltpu.VMEM((n_exp, V7X_LANES), jnp.float32),
        ],
        compiler_params=pltpu.CompilerParams(
            dimension_semantics=("arbitrary", "arbitrary"),
            vmem_limit_bytes=V7X_VMEM_LIMIT_BYTES),
    )(x, mod_l, gmix, win_bf, conv_w, wsp, bsp_full, lng, wout_bf, gffn, rwt, rb)


def _dispatch_kernel(dest_ref, h2_ref, xs_ref, sem):
    tmb = h2_ref.shape[0]

    def row_copy(t, k):
        d = dest_ref[t * TOP_K + k]
        return pltpu.make_async_copy(h2_ref.at[pl.ds(t, 1)], xs_ref.at[pl.ds(d, 1)], sem)

    def issue(t, carry):
        for k in range(TOP_K):
            row_copy(t, k).start()
        return carry

    def drain(t, carry):
        for k in range(TOP_K):
            row_copy(t, k).wait()
        return carry

    lax.fori_loop(0, tmb, issue, 0)
    lax.fori_loop(0, tmb, drain, 0)


def _dispatch(dest_flat, h2):
    n_tok, d = h2.shape
    tmb = DISPATCH_TILE
    return pl.pallas_call(
        _dispatch_kernel,
        grid=(n_tok // tmb,),
        in_specs=[
            pl.BlockSpec((tmb * TOP_K,), lambda i: (i,), memory_space=pltpu.SMEM),
            pl.BlockSpec((tmb, d), lambda i: (i, 0)),
        ],
        out_specs=pl.BlockSpec(memory_space=pl.ANY),
        out_shape=jax.ShapeDtypeStruct((n_tok * TOP_K, d), jnp.float32),
        scratch_shapes=[pltpu.SemaphoreType.DMA(())],
        compiler_params=pltpu.CompilerParams(dimension_semantics=("arbitrary",)),
    )(dest_flat, h2)


def _experts_kernel(blk_ref, exp_ref, lo_ref, hi_ref,
                    xs_ref, wgu_ref, bgu_ref, wdn_ref, bdn_ref, y_ref, wgu_bf, wdn_bf):
    i = pl.program_id(0)
    rows, d = xs_ref.shape
    d_ff = wdn_ref.shape[0]
    prev = jnp.maximum(i - 1, 0)
    new_expert = jnp.logical_or(i == 0, exp_ref[i] != exp_ref[prev])
    new_block = jnp.logical_or(i == 0, blk_ref[i] != blk_ref[prev])
    lo = lo_ref[i]
    hi = hi_ref[i]

    @pl.when(new_expert)
    def _():
        wgu_bf[...] = wgu_ref[...].astype(jnp.bfloat16)
        wdn_bf[...] = wdn_ref[...].astype(jnp.bfloat16)

    def compute():
        xs = xs_ref[...].astype(jnp.bfloat16)
        gu = jnp.dot(xs, wgu_bf[...], preferred_element_type=jnp.float32) + bgu_ref[...]
        gate = jnp.minimum(gu[:, :d_ff], SWIGLU_LIMIT)
        up = jnp.clip(gu[:, d_ff:], -SWIGLU_LIMIT, SWIGLU_LIMIT)
        act = gate * _sigmoid(SWIGLU_ALPHA * gate) * (up + 1.0)
        y = jnp.dot(act.astype(jnp.bfloat16), wdn_bf[...],
                    preferred_element_type=jnp.float32) + bdn_ref[...]
        ridx = lax.broadcasted_iota(jnp.int32, (rows, d), 0)
        return y, jnp.logical_and(ridx >= lo, ridx < hi)

    @pl.when(jnp.logical_and(hi > lo, new_block))
    def _():
        y, mine = compute()
        y_ref[...] = jnp.where(mine, y, 0.0)

    @pl.when(jnp.logical_and(hi > lo, jnp.logical_not(new_block)))
    def _():
        y, mine = compute()
        y_ref[...] = jnp.where(mine, y, y_ref[...])


def _experts(items, xs, w_gu, b_gu, w_dn, b_dn, layer):
    n_rows, d = xs.shape
    blk, exp, lo, hi = items
    n_items = blk.shape[0]
    rows = EXPERT_BLOCK
    d_ff2 = w_gu.shape[-1]
    d_ff = w_dn.shape[-2]
    n_exp = w_gu.shape[1]
    grid_spec = pltpu.PrefetchScalarGridSpec(
        num_scalar_prefetch=4,
        grid=(n_items,),
        in_specs=[
            pl.BlockSpec((rows, d), lambda i, blk, exp, lo, hi: (blk[i], 0)),
            pl.BlockSpec((None, None, d, d_ff2), lambda i, blk, exp, lo, hi: (layer, exp[i], 0, 0)),
            pl.BlockSpec((None, None, 1, d_ff2), lambda i, blk, exp, lo, hi: (layer, exp[i], 0, 0)),
            pl.BlockSpec((None, None, d_ff, d), lambda i, blk, exp, lo, hi: (layer, exp[i], 0, 0)),
            pl.BlockSpec((None, None, 1, d), lambda i, blk, exp, lo, hi: (layer, exp[i], 0, 0)),
        ],
        out_specs=pl.BlockSpec((rows, d), lambda i, blk, exp, lo, hi: (blk[i], 0)),
        scratch_shapes=[
            pltpu.VMEM((d, d_ff2), jnp.bfloat16),
            pltpu.VMEM((d_ff, d), jnp.bfloat16),
        ],
    )
    depth = w_gu.shape[0]
    return pl.pallas_call(
        _experts_kernel,
        grid_spec=grid_spec,
        out_shape=jax.ShapeDtypeStruct((n_rows, d), jnp.float32),
        compiler_params=pltpu.CompilerParams(
            dimension_semantics=("arbitrary",),
            vmem_limit_bytes=V7X_VMEM_LIMIT_BYTES),
    )(blk, exp, lo, hi, xs, w_gu, b_gu.reshape(depth, n_exp, 1, d_ff2), w_dn,
      b_dn.reshape(depth, n_exp, 1, d))


def _expert_items(counts, n_rows):
    n_exp = counts.shape[0]
    rows = EXPERT_BLOCK
    n_blk = n_rows // rows
    n_items = n_blk + n_exp - 1
    grp_end = jnp.cumsum(counts)
    grp_start = grp_end - counts
    blk_lo = jnp.arange(n_blk, dtype=jnp.int32) * rows
    first_e = jnp.searchsorted(grp_end, blk_lo, side="right").astype(jnp.int32)
    last_e = jnp.searchsorted(grp_end, blk_lo + (rows - 1), side="right").astype(jnp.int32)
    per_blk = last_e - first_e + 1
    item_end = jnp.cumsum(per_blk)
    item_start = item_end - per_blk
    total = item_end[-1]
    idx = jnp.arange(n_items, dtype=jnp.int32)
    used = idx < total
    blk = jnp.minimum(jnp.searchsorted(item_end, idx, side="right").astype(jnp.int32), n_blk - 1)
    exp = jnp.where(used, first_e[blk] + idx - item_start[blk], last_e[n_blk - 1])
    base = blk * rows
    lo = jnp.clip(grp_start[exp] - base, 0, rows)
    hi = jnp.clip(grp_end[exp] - base, 0, rows)
    hi = jnp.where(used, hi, lo)
    return (blk.astype(jnp.int32), exp.astype(jnp.int32), lo.astype(jnp.int32),
            hi.astype(jnp.int32))


def _combine_kernel(dest_ref, x1_ref, w_ref, mod_ref, fin_ref, y_ref, o_ref, gbuf, sem,
                    *, final_norm):
    tmd, d = x1_ref.shape

    def row_copy(t, k):
        src = dest_ref[t * TOP_K + k]
        return pltpu.make_async_copy(y_ref.at[pl.ds(src, 1)], gbuf.at[k, pl.ds(t, 1)], sem)

    def issue(t, carry):
        for k in range(TOP_K):
            row_copy(t, k).start()
        return carry

    def drain(t, carry):
        for k in range(TOP_K):
            row_copy(t, k).wait()
        return carry

    lax.fori_loop(0, tmd, issue, 0)
    lax.fori_loop(0, tmd, drain, 0)

    w = w_ref[...]
    acc = w[:, 0:1] * gbuf[0]
    for k in range(1, TOP_K):
        acc = acc + w[:, k:k + 1] * gbuf[k]
    g_f = mod_ref[:, 5 * d:6 * d]
    x2 = x1_ref[...] + g_f * acc
    if final_norm:
        r = lax.rsqrt(jnp.mean(x2 * x2, axis=-1, keepdims=True) + NORM_EPS)
        x2 = x2 * r * fin_ref[...]
    o_ref[...] = x2


def _combine(dest_flat, x1, w_tok, mod_l, fin_g, y, final_norm):
    bsz, seq, d = x1.shape
    tmd = COMBINE_TILE
    tiles = seq // tmd
    return pl.pallas_call(
        functools.partial(_combine_kernel, final_norm=final_norm),
        grid=(bsz, tiles),
        in_specs=[
            pl.BlockSpec((tmd * TOP_K,), lambda b, s: (b * tiles + s,), memory_space=pltpu.SMEM),
            pl.BlockSpec((None, tmd, d), lambda b, s: (b, s, 0)),
            pl.BlockSpec((tmd, TOP_K), lambda b, s: (b * tiles + s, 0)),
            pl.BlockSpec((None, 1, 6 * d), lambda b, s: (b, 0, 0)),
            pl.BlockSpec((1, d), lambda b, s: (0, 0)),
            pl.BlockSpec(memory_space=pl.ANY),
        ],
        out_specs=pl.BlockSpec((None, tmd, d), lambda b, s: (b, s, 0)),
        out_shape=jax.ShapeDtypeStruct((bsz, seq, d), jnp.float32),
        scratch_shapes=[
            pltpu.VMEM((TOP_K, tmd, d), jnp.float32),
            pltpu.SemaphoreType.DMA(()),
        ],
        compiler_params=pltpu.CompilerParams(dimension_semantics=("arbitrary", "arbitrary")),
    )(dest_flat, x1, w_tok, mod_l, fin_g, y)


def kernel(x, c, w_ada, b_ada, norm_mix_g, w_in, conv_w, w_spatial, b_spatial, ln_v_g, w_out,
           norm_ffn_g, router_w, router_b, w_gate_up, b_gate_up, w_down, b_down, final_norm_g):
    bsz, seq, d = x.shape
    depth = w_ada.shape[0]
    n_exp = router_w.shape[-1]
    n_tok = bsz * seq
    assert seq % MIX_TILE == 0 and MIX_TILE % CHUNK == 0
    assert seq % COMBINE_TILE == 0 and n_tok % DISPATCH_TILE == 0
    assert (n_tok * TOP_K) % EXPERT_BLOCK == 0

    mod = _modulation(c, w_ada, b_ada)
    win_bf = w_in.astype(jnp.bfloat16)
    wout_bf = w_out.astype(jnp.bfloat16)
    bsp_full = jnp.repeat(jnp.swapaxes(b_spatial, 1, 2), d // GROUPS, axis=2)
    rwt = jnp.swapaxes(router_w, 1, 2)
    fin_g = final_norm_g.reshape(1, d)

    for l in range(depth):
        x1, h2, top_e, top_w, rank, cnt = _mixer(
            x, mod[l], norm_mix_g[l].reshape(1, d), win_bf[l], conv_w[l], w_spatial[l],
            bsp_full[l], ln_v_g[l].reshape(1, d), wout_bf[l], norm_ffn_g[l].reshape(1, d),
            rwt[l], router_b[l].reshape(n_exp, 1))
        counts = cnt[:, 0]
        grp_start = jnp.cumsum(counts) - counts
        dest = rank
        for e in range(n_exp):
            dest = dest + jnp.where(top_e == e, grp_start[e], 0)
        dest_flat = dest.T.reshape(-1)
        xs = _dispatch(dest_flat, h2)
        items = _expert_items(counts, n_tok * TOP_K)
        y = _experts(items, xs, w_gate_up, b_gate_up, w_down, b_down, l)
        x = _combine(dest_flat, x1, top_w.T, mod[l], fin_g, y, final_norm=(l == depth - 1))
    return x
```

```python
import functools
import math

import jax
import jax.numpy as jnp
from jax import lax
from jax.experimental import pallas as pl
from jax.experimental.pallas import tpu as pltpu

CONV_K = 3
GROUPS = 8
CHUNK = 128
TOP_K = 4
SWIGLU_LIMIT = 7.0
SWIGLU_ALPHA = 1.702
NORM_EPS = 1e-6

V7X_SUBLANES = 8
V7X_LANES = 128
V7X_VMEM_LIMIT_BYTES = 56 * 1024 * 1024

MIX_TILE = 512
ROW_CHUNK = CHUNK
COMBINE_TILE = 256
EXPERT_BLOCK = 512


def _split_bf16(a):
    hi = a.astype(jnp.bfloat16)
    lo = (a - hi.astype(jnp.float32)).astype(jnp.bfloat16)
    return hi, lo


def _gelu(a):
    return 0.5 * a * (1.0 + lax.erf(a * (1.0 / math.sqrt(2.0))))


def _sigmoid(a):
    return 1.0 / (1.0 + jnp.exp(-a))


def _mod_kernel(c_ref, w_ref, b_ref, o_ref):
    c = c_ref[...]
    s = c * _sigmoid(c)
    s_hi, s_lo = _split_bf16(s)
    w_hi, w_lo = _split_bf16(w_ref[...])
    dot = functools.partial(jnp.dot, preferred_element_type=jnp.float32)
    o_ref[...] = dot(s_hi, w_hi) + dot(s_lo, w_hi) + dot(s_hi, w_lo) + b_ref[...]


def _modulation(c, w_ada, b_ada):
    depth, d, d6 = w_ada.shape
    bsz = c.shape[0]
    rows = -(-bsz // V7X_SUBLANES) * V7X_SUBLANES
    c_pad = jnp.zeros((rows, d), jnp.float32).at[:bsz].set(c)
    n_col = d6 // d
    out = pl.pallas_call(
        _mod_kernel,
        grid=(depth, n_col),
        in_specs=[
            pl.BlockSpec((rows, d), lambda l, j: (0, 0)),
            pl.BlockSpec((None, d, d), lambda l, j: (l, 0, j)),
            pl.BlockSpec((None, 1, d), lambda l, j: (l, 0, j)),
        ],
        out_specs=pl.BlockSpec((None, rows, d), lambda l, j: (l, 0, j)),
        out_shape=jax.ShapeDtypeStruct((depth, rows, d6), jnp.float32),
        name="adaln_modulation",
    )(c_pad, w_ada, b_ada.reshape(depth, 1, d6))
    return out[:, :bsz].reshape(depth, bsz, 1, d6)


def _mixer_kernel(x_ref, mod_ref, gmix_ref, win_ref, conv_ref, wsp_ref, bsp_ref, lng_ref,
                  wout_ref, gffn_ref, rwt_ref, rb_ref,
                  x1_ref, dest_ref, tw_ref, cnt_ref, xs_ref,
                  hbuf, zbuf, halo_x, halo_c, mabuf, mbuf, run_ref, h2buf, dest_v, dest_s,
                  dest_sem, row_sem, *, capacity):
    tm, d = x_ref.shape
    n_exp = rwt_ref.shape[0]
    n_chunks = tm // ROW_CHUNK
    step = pl.program_id(0) * pl.num_programs(1) + pl.program_id(1)
    n_steps = pl.num_programs(0) * pl.num_programs(1)
    slot = step % 2
    dot = functools.partial(jnp.dot, preferred_element_type=jnp.float32)

    @pl.when(pl.program_id(1) == 0)
    def _():
        halo_x[...] = jnp.zeros_like(halo_x)
        halo_c[...] = jnp.zeros_like(halo_c)

    @pl.when(step == 0)
    def _():
        run_ref[...] = jnp.zeros_like(run_ref)

    mod = mod_ref[...]
    sh_m, sc_m, g_m = mod[:, 0:d], mod[:, d:2 * d], mod[:, 2 * d:3 * d]
    sh_f, sc_f = mod[:, 3 * d:4 * d], mod[:, 4 * d:5 * d]

    x = x_ref[...]
    gs = gmix_ref[...] * (1.0 + sc_m)
    r = lax.rsqrt(jnp.mean(x * x, axis=-1, keepdims=True) + NORM_EPS)
    hbuf[...] = (x * r * gs + sh_m).astype(jnp.bfloat16)

    h = hbuf[...]
    zbuf[:, 0:3 * d] = dot(h, win_ref[:, 0:3 * d])
    zbuf[:, 3 * d:4 * d] = dot(h, win_ref[:, 5 * d:6 * d])

    cw = conv_ref[...]
    w0, w1, w2 = cw[0:1, :], cw[1:2, :], cw[2:3, :]
    row8 = lax.broadcasted_iota(jnp.int32, (V7X_SUBLANES, d), 0)

    def branch_a(i, carry):
        r0 = pl.multiple_of(i * ROW_CHUNK, ROW_CHUNK)
        rows = pl.ds(r0, ROW_CHUNK)
        p0 = zbuf[rows, 0:d] * zbuf[rows, 2 * d:3 * d]
        rp = pl.ds(pl.multiple_of(jnp.maximum(r0 - V7X_SUBLANES, 0), V7X_SUBLANES), V7X_SUBLANES)
        prev_in = zbuf[rp, 0:d] * zbuf[rp, 2 * d:3 * d]
        prev = jnp.where(i == 0, halo_x[...] * halo_c[...], prev_in)
        r1 = pltpu.roll(p0, 1, 0)
        r2 = pltpu.roll(p0, 2, 0)
        q1 = pltpu.roll(prev, 1, 0)
        q2 = pltpu.roll(prev, 2, 0)
        p1 = jnp.concatenate([jnp.where(row8 < 1, q1, r1[0:V7X_SUBLANES]), r1[V7X_SUBLANES:]], axis=0)
        p2 = jnp.concatenate([jnp.where(row8 < 2, q2, r2[0:V7X_SUBLANES]), r2[V7X_SUBLANES:]], axis=0)
        y_a = zbuf[rows, d:2 * d] * (w0 * p2 + w1 * p1 + w2 * p0)
        mabuf[rows, :] = _sigmoid(zbuf[rows, 3 * d:4 * d]) * y_a
        return carry

    lax.fori_loop(0, n_chunks, branch_a, 0)
    last8 = pl.ds(tm - V7X_SUBLANES, V7X_SUBLANES)
    halo_x[...] = zbuf[last8, 0:d]
    halo_c[...] = zbuf[last8, 2 * d:3 * d]

    zbuf[:, 0:2 * d] = dot(h, win_ref[:, 3 * d:5 * d])
    zbuf[:, 2 * d:3 * d] = dot(h, win_ref[:, 6 * d:7 * d])

    tri = (lax.broadcasted_iota(jnp.int32, (CHUNK, CHUNK), 1)
           <= lax.broadcasted_iota(jnp.int32, (CHUNK, CHUNK), 0))
    wm = [jnp.where(tri, wsp_ref[g], 0.0).astype(jnp.bfloat16) for g in range(GROUPS)]
    gd = d // GROUPS
    lng = lng_ref[...]

    def branch_b(i, carry):
        r0 = pl.multiple_of(i * ROW_CHUNK, ROW_CHUNK)
        rows = pl.ds(r0, ROW_CHUNK)
        gv = _gelu(zbuf[rows, d:2 * d])
        mu = jnp.mean(gv, axis=-1, keepdims=True)
        vc = gv - mu
        var = jnp.mean(vc * vc, axis=-1, keepdims=True)
        vn = (vc * lax.rsqrt(var + NORM_EPS) * lng).astype(jnp.bfloat16)
        s = jnp.concatenate(
            [dot(wm[g], vn[:, g * gd:(g + 1) * gd]) for g in range(GROUPS)], axis=1)
        y_b = _gelu(zbuf[rows, 0:d]) * (s + bsp_ref[...])
        m = mabuf[rows, :] + _sigmoid(zbuf[rows, 2 * d:3 * d]) * y_b
        mbuf[rows, :] = m.astype(jnp.bfloat16)
        return carry

    lax.fori_loop(0, n_chunks, branch_b, 0)

    x1 = x_ref[...] + g_m * dot(mbuf[...], wout_ref[...])
    x1_ref[...] = x1
    gs2 = gffn_ref[...] * (1.0 + sc_f)
    r2n = lax.rsqrt(jnp.mean(x1 * x1, axis=-1, keepdims=True) + NORM_EPS)
    h2 = x1 * r2n * gs2 + sh_f

    def tile_rows_wait(which):
        for _ in range(TOP_K):
            pltpu.make_async_copy(h2buf.at[which], xs_ref.at[pl.ds(0, tm)], row_sem).wait()

    @pl.when(step > 0)
    def _():
        tile_rows_wait(1 - slot)

    h2buf[slot] = h2

    nt = (((1,), (1,)), ((), ()))
    dg = functools.partial(lax.dot_general, dimension_numbers=nt,
                           preferred_element_type=jnp.float32)
    h_hi, h_lo = _split_bf16(h2)
    r_hi, r_lo = _split_bf16(rwt_ref[...])
    logits = dg(r_hi, h_hi) + dg(r_hi, h_lo) + dg(r_lo, h_hi) + rb_ref[...]

    iota_e = lax.broadcasted_iota(jnp.int32, (n_exp, tm), 0)
    vals = logits
    tops, idxs, hots = [], [], []
    for _ in range(TOP_K):
        mx = jnp.max(vals, axis=0, keepdims=True)
        ix = jnp.min(jnp.where(vals == mx, iota_e, n_exp), axis=0, keepdims=True)
        hot = iota_e == ix
        vals = jnp.where(hot, -jnp.inf, vals)
        tops.append(mx)
        idxs.append(ix)
        hots.append(hot)
    exps = [jnp.exp(t - tops[0]) for t in tops]
    denom = exps[0] + exps[1] + exps[2] + exps[3]
    for k in range(TOP_K):
        tw_ref[k:k + 1, :] = exps[k] / denom

    member = jnp.where(hots[0] | hots[1] | hots[2] | hots[3], 1.0, 0.0)
    upper = (lax.broadcasted_iota(jnp.int32, (tm, tm), 0)
             < lax.broadcasted_iota(jnp.int32, (tm, tm), 1))
    before = dot(member.astype(jnp.bfloat16), jnp.where(upper, 1.0, 0.0).astype(jnp.bfloat16))
    run = run_ref[...]
    pos = before + run[:, 0:1]
    for k in range(TOP_K):
        rk = jnp.sum(jnp.where(hots[k], pos, 0.0), axis=0, keepdims=True)
        dest_k = idxs[k] * capacity + rk.astype(jnp.int32)
        dest_ref[k:k + 1, :] = dest_k
        dest_v[k:k + 1, :] = dest_k
    run = run + jnp.sum(member, axis=1, keepdims=True)
    run_ref[...] = run
    cnt_ref[...] = run.astype(jnp.int32)

    dest_copy = pltpu.make_async_copy(dest_v, dest_s, dest_sem)
    dest_copy.start()
    dest_copy.wait()

    def issue(t, carry):
        for k in range(TOP_K):
            pltpu.make_async_copy(h2buf.at[slot, pl.ds(t, 1)],
                                  xs_ref.at[pl.ds(dest_s[k, t], 1)], row_sem).start()
        return carry

    lax.fori_loop(0, tm, issue, 0)

    @pl.when(step == n_steps - 1)
    def _():
        tile_rows_wait(slot)


def _mixer(x, mod_l, gmix, win_bf, conv_w, wsp, bsp_full, lng, wout_bf, gffn, rwt, rb):
    bsz, seq, d = x.shape
    n_exp = rwt.shape[0]
    tm = MIX_TILE
    n_tok = bsz * seq
    tiles = seq // tm
    const = pl.Buffered(1)

    def cspec(shape):
        return pl.BlockSpec(shape, lambda b, s: (0,) * len(shape), pipeline_mode=const)

    out_shape = (
        jax.ShapeDtypeStruct((bsz, seq, d), jnp.float32),
        jax.ShapeDtypeStruct((TOP_K, n_tok), jnp.int32),
        jax.ShapeDtypeStruct((TOP_K, n_tok), jnp.float32),
        jax.ShapeDtypeStruct((n_exp, V7X_LANES), jnp.int32),
        jax.ShapeDtypeStruct((n_exp * n_tok, d), jnp.float32),
    )
    tok_t = lambda b, s: (0, b * tiles + s)
    return pl.pallas_call(
        functools.partial(_mixer_kernel, capacity=n_tok),
        grid=(bsz, tiles),
        in_specs=[
            pl.BlockSpec((None, tm, d), lambda b, s: (b, s, 0)),
            pl.BlockSpec((None, 1, 6 * d), lambda b, s: (b, 0, 0)),
            cspec((1, d)),
            cspec(win_bf.shape),
            cspec(conv_w.shape),
            cspec(wsp.shape),
            cspec(bsp_full.shape),
            cspec((1, d)),
            cspec(wout_bf.shape),
            cspec((1, d)),
            cspec(rwt.shape),
            cspec(rb.shape),
        ],
        out_specs=(
            pl.BlockSpec((None, tm, d), lambda b, s: (b, s, 0)),
            pl.BlockSpec((TOP_K, tm), tok_t),
            pl.BlockSpec((TOP_K, tm), tok_t),
            pl.BlockSpec((n_exp, V7X_LANES), lambda b, s: (0, 0)),
            pl.BlockSpec(memory_space=pl.ANY),
        ),
        out_shape=out_shape,
        scratch_shapes=[
            pltpu.VMEM((tm, d), jnp.bfloat16),
            pltpu.VMEM((tm, 4 * d), jnp.float32),
            pltpu.VMEM((V7X_SUBLANES, d), jnp.float32),
            pltpu.VMEM((V7X_SUBLANES, d), jnp.float32),
            pltpu.VMEM((tm, d), jnp.float32),
            pltpu.VMEM((tm, d), jnp.bfloat16),
            pltpu.VMEM((n_exp, V7X_LANES), jnp.float32),
            pltpu.VMEM((2, tm, d), jnp.float32),
            pltpu.VMEM((TOP_K, tm), jnp.int32),
            pltpu.SMEM((TOP_K, tm), jnp.int32),
            pltpu.SemaphoreType.DMA(()),
            pltpu.SemaphoreType.DMA(()),
        ],
        compiler_params=pltpu.CompilerParams(
            dimension_semantics=("arbitrary", "arbitrary"),
            vmem_limit_bytes=V7X_VMEM_LIMIT_BYTES),
        name="mixer_router_dispatch",
    )(x, mod_l, gmix, win_bf, conv_w, wsp, bsp_full, lng, wout_bf, gffn, rwt, rb)


def _experts_kernel(blk_ref, exp_ref, nv_ref,
                    xs_ref, wgu_ref, bgu_ref, wdn_ref, bdn_ref, y_ref, wgu_bf, wdn_bf):
    i = pl.program_id(0)
    rows, d = xs_ref.shape
    d_ff = wdn_ref.shape[0]
    prev = jnp.maximum(i - 1, 0)
    new_expert = jnp.logical_or(i == 0, exp_ref[i] != exp_ref[prev])
    n_valid = nv_ref[i]

    @pl.when(new_expert)
    def _():
        wgu_bf[...] = wgu_ref[...].astype(jnp.bfloat16)
        wdn_bf[...] = wdn_ref[...].astype(jnp.bfloat16)

    @pl.when(n_valid > 0)
    def _():
        ridx = lax.broadcasted_iota(jnp.int32, (rows, d), 0)
        xs = jnp.where(ridx < n_valid, xs_ref[...], 0.0).astype(jnp.bfloat16)
        gu = jnp.dot(xs, wgu_bf[...], preferred_element_type=jnp.float32) + bgu_ref[...]
        gate = jnp.minimum(gu[:, :d_ff], SWIGLU_LIMIT)
        up = jnp.clip(gu[:, d_ff:], -SWIGLU_LIMIT, SWIGLU_LIMIT)
        act = gate * _sigmoid(SWIGLU_ALPHA * gate) * (up + 1.0)
        y_ref[...] = jnp.dot(act.astype(jnp.bfloat16), wdn_bf[...],
                             preferred_element_type=jnp.float32) + bdn_ref[...]


def _experts(items, xs, w_gu, b_gu, w_dn, b_dn, layer):
    n_rows, d = xs.shape
    blk, exp, n_valid = items
    n_items = blk.shape[0]
    rows = EXPERT_BLOCK
    d_ff2 = w_gu.shape[-1]
    d_ff = w_dn.shape[-2]
    n_exp = w_gu.shape[1]
    depth = w_gu.shape[0]
    wmap = lambda i, blk, exp, nv: (layer, exp[i], 0, 0)
    grid_spec = pltpu.PrefetchScalarGridSpec(
        num_scalar_prefetch=3,
        grid=(n_items,),
        in_specs=[
            pl.BlockSpec((rows, d), lambda i, blk, exp, nv: (blk[i], 0)),
            pl.BlockSpec((None, None, d, d_ff2), wmap),
            pl.BlockSpec((None, None, 1, d_ff2), wmap),
            pl.BlockSpec((None, None, d_ff, d), wmap),
            pl.BlockSpec((None, None, 1, d), wmap),
        ],
        out_specs=pl.BlockSpec((rows, d), lambda i, blk, exp, nv: (blk[i], 0)),
        scratch_shapes=[
            pltpu.VMEM((d, d_ff2), jnp.bfloat16),
            pltpu.VMEM((d_ff, d), jnp.bfloat16),
        ],
    )
    return pl.pallas_call(
        _experts_kernel,
        grid_spec=grid_spec,
        out_shape=jax.ShapeDtypeStruct((n_rows, d), jnp.float32),
        compiler_params=pltpu.CompilerParams(
            dimension_semantics=("arbitrary",),
            vmem_limit_bytes=V7X_VMEM_LIMIT_BYTES),
        name="experts",
    )(blk, exp, n_valid, xs, w_gu, b_gu.reshape(depth, n_exp, 1, d_ff2), w_dn,
      b_dn.reshape(depth, n_exp, 1, d))


def _expert_items(counts, capacity, n_assign):
    n_exp = counts.shape[0]
    rows = EXPERT_BLOCK
    n_items = n_assign // rows + n_exp
    per_exp = (counts + rows - 1) // rows
    item_end = jnp.cumsum(per_exp)
    item_start = item_end - per_exp
    total = item_end[-1]
    idx = jnp.arange(n_items, dtype=jnp.int32)
    pick = jnp.minimum(idx, total - 1)
    exp = jnp.sum((item_end[None, :] <= pick[:, None]).astype(jnp.int32), axis=1)
    sub = pick - item_start[exp]
    blk = exp * (capacity // rows) + sub
    n_valid = jnp.where(idx < total, jnp.minimum(counts[exp] - sub * rows, rows), 0)
    return blk.astype(jnp.int32), exp.astype(jnp.int32), n_valid.astype(jnp.int32)


def _combine_kernel(dest_ref, x1_ref, w_ref, mod_ref, fin_ref, y_ref, o_ref, gbuf, sem,
                    *, final_norm):
    tmd, d = x1_ref.shape

    def row_copy(t, k):
        src = dest_ref[t * TOP_K + k]
        return pltpu.make_async_copy(y_ref.at[pl.ds(src, 1)], gbuf.at[k, pl.ds(t, 1)], sem)

    def issue(t, carry):
        for k in range(TOP_K):
            row_copy(t, k).start()
        return carry

    lax.fori_loop(0, tmd, issue, 0)
    for k in range(TOP_K):
        pltpu.make_async_copy(y_ref.at[pl.ds(0, tmd)], gbuf.at[k], sem).wait()

    w = w_ref[...]
    acc = w[:, 0:1] * gbuf[0]
    for k in range(1, TOP_K):
        acc = acc + w[:, k:k + 1] * gbuf[k]
    g_f = mod_ref[:, 5 * d:6 * d]
    x2 = x1_ref[...] + g_f * acc
    if final_norm:
        r = lax.rsqrt(jnp.mean(x2 * x2, axis=-1, keepdims=True) + NORM_EPS)
        x2 = x2 * r * fin_ref[...]
    o_ref[...] = x2


def _combine(dest_flat, x1, w_tok, mod_l, fin_g, y, final_norm):
    bsz, seq, d = x1.shape
    tmd = COMBINE_TILE
    tiles = seq // tmd
    return pl.pallas_call(
        functools.partial(_combine_kernel, final_norm=final_norm),
        grid=(bsz, tiles),
        in_specs=[
            pl.BlockSpec((tmd * TOP_K,), lambda b, s: (b * tiles + s,), memory_space=pltpu.SMEM),
            pl.BlockSpec((None, tmd, d), lambda b, s: (b, s, 0)),
            pl.BlockSpec((tmd, TOP_K), lambda b, s: (b * tiles + s, 0)),
            pl.BlockSpec((None, 1, 6 * d), lambda b, s: (b, 0, 0)),
            pl.BlockSpec((1, d), lambda b, s: (0, 0)),
            pl.BlockSpec(memory_space=pl.ANY),
        ],
        out_specs=pl.BlockSpec((None, tmd, d), lambda b, s: (b, s, 0)),
        out_shape=jax.ShapeDtypeStruct((bsz, seq, d), jnp.float32),
        scratch_shapes=[
            pltpu.VMEM((TOP_K, tmd, d), jnp.float32),
            pltpu.SemaphoreType.DMA(()),
        ],
        compiler_params=pltpu.CompilerParams(dimension_semantics=("arbitrary", "arbitrary")),
        name="combine",
    )(dest_flat, x1, w_tok, mod_l, fin_g, y)


def kernel(x, c, w_ada, b_ada, norm_mix_g, w_in, conv_w, w_spatial, b_spatial, ln_v_g, w_out,
           norm_ffn_g, router_w, router_b, w_gate_up, b_gate_up, w_down, b_down, final_norm_g):
    bsz, seq, d = x.shape
    depth = w_ada.shape[0]
    n_exp = router_w.shape[-1]
    n_tok = bsz * seq
    assert seq % MIX_TILE == 0 and MIX_TILE % CHUNK == 0
    assert seq % COMBINE_TILE == 0 and n_tok % EXPERT_BLOCK == 0

    mod = _modulation(c, w_ada, b_ada)
    win_bf = w_in.astype(jnp.bfloat16)
    wout_bf = w_out.astype(jnp.bfloat16)
    bsp_full = jnp.repeat(jnp.swapaxes(b_spatial, 1, 2), d // GROUPS, axis=2)
    rwt = jnp.swapaxes(router_w, 1, 2)
    fin_g = final_norm_g.reshape(1, d)

    for l in range(depth):
        x1, dest, top_w, cnt, xs = _mixer(
            x, mod[l], norm_mix_g[l].reshape(1, d), win_bf[l], conv_w[l], w_spatial[l],
            bsp_full[l], ln_v_g[l].reshape(1, d), wout_bf[l], norm_ffn_g[l].reshape(1, d),
            rwt[l], router_b[l].reshape(n_exp, 1))
        items = _expert_items(cnt[:, 0], n_tok, n_tok * TOP_K)
        y = _experts(items, xs, w_gate_up, b_gate_up, w_down, b_down, l)
        x = _combine(dest.T.reshape(-1), x1, top_w.T, mod[l], fin_g, y,
                     final_norm=(l == depth - 1))
    return x
```

```python
import functools
import math

import jax
import jax.numpy as jnp
from jax import lax
from jax.experimental import pallas as pl
from jax.experimental.pallas import tpu as pltpu

CONV_K = 3
GROUPS = 8
CHUNK = 128
TOP_K = 4
SWIGLU_LIMIT = 7.0
SWIGLU_ALPHA = 1.702
NORM_EPS = 1e-6

V7X_SUBLANES = 8
V7X_LANES = 128
V7X_VMEM_LIMIT_BYTES = 56 * 1024 * 1024

MIX_TILE = 512
ROW_CHUNK = CHUNK
COMBINE_TILE = 256
EXPERT_BLOCK = 512


def _split_bf16(a):
    hi = a.astype(jnp.bfloat16)
    lo = (a - hi.astype(jnp.float32)).astype(jnp.bfloat16)
    return hi, lo


def _gelu(a):
    return 0.5 * a * (1.0 + lax.erf(a * (1.0 / math.sqrt(2.0))))


def _sigmoid(a):
    return 1.0 / (1.0 + jnp.exp(-a))


def _mod_kernel(c_ref, w_ref, b_ref, o_ref):
    c = c_ref[...]
    s = c * _sigmoid(c)
    s_hi, s_lo = _split_bf16(s)
    w_hi, w_lo = _split_bf16(w_ref[...])
    dot = functools.partial(jnp.dot, preferred_element_type=jnp.float32)
    o_ref[...] = dot(s_hi, w_hi) + dot(s_lo, w_hi) + dot(s_hi, w_lo) + b_ref[...]


def _modulation(c, w_ada, b_ada):
    depth, d, d6 = w_ada.shape
    bsz = c.shape[0]
    rows = -(-bsz // V7X_SUBLANES) * V7X_SUBLANES
    c_pad = jnp.zeros((rows, d), jnp.float32).at[:bsz].set(c)
    n_col = d6 // d
    out = pl.pallas_call(
        _mod_kernel,
        grid=(depth, n_col),
        in_specs=[
            pl.BlockSpec((rows, d), lambda l, j: (0, 0)),
            pl.BlockSpec((None, d, d), lambda l, j: (l, 0, j)),
            pl.BlockSpec((None, 1, d), lambda l, j: (l, 0, j)),
        ],
        out_specs=pl.BlockSpec((None, rows, d), lambda l, j: (l, 0, j)),
        out_shape=jax.ShapeDtypeStruct((depth, rows, d6), jnp.float32),
        name="adaln_modulation",
    )(c_pad, w_ada, b_ada.reshape(depth, 1, d6))
    return out[:, :bsz].reshape(depth, bsz, 1, d6)


def _mixer_kernel(x_ref, mod_ref, gmix_ref, win_ref, conv_ref, wsp_ref, bsp_ref, lng_ref,
                  wout_ref, gffn_ref, rwt_ref, rb_ref,
                  x1_ref, dest_ref, tw_ref, cnt_ref, xs_ref,
                  hbuf, zbuf, halo_x, halo_c, mabuf, mbuf, run_ref, h2buf, dest_v, dest_s,
                  dest_sem, row_sem, *, capacity):
    tm, d = x_ref.shape
    n_exp = rwt_ref.shape[0]
    n_chunks = tm // ROW_CHUNK
    step = pl.program_id(0) * pl.num_programs(1) + pl.program_id(1)
    n_steps = pl.num_programs(0) * pl.num_programs(1)
    slot = step % 2
    dot = functools.partial(jnp.dot, preferred_element_type=jnp.float32)

    @pl.when(pl.program_id(1) == 0)
    def _():
        halo_x[...] = jnp.zeros_like(halo_x)
        halo_c[...] = jnp.zeros_like(halo_c)

    def row_copy(which, t, k, row):
        return pltpu.make_async_copy(h2buf.at[which, pl.ds(t, 1)], xs_ref.at[pl.ds(row, 1)],
                                     row_sem.at[which])

    def tile_rows_wait(which):
        for _ in range(TOP_K):
            pltpu.make_async_copy(h2buf.at[which], xs_ref.at[pl.ds(0, tm)],
                                  row_sem.at[which]).wait()

    def dest_to_smem():
        dest_copy = pltpu.make_async_copy(dest_v, dest_s, dest_sem)
        dest_copy.start()
        dest_copy.wait()

    @pl.when(step == 0)
    def _():
        run_ref[...] = jnp.zeros_like(run_ref)
        h2buf[1] = jnp.zeros((tm, d), jnp.float32)
        spare = (n_exp * capacity + lax.broadcasted_iota(jnp.int32, (TOP_K, tm), 0) * tm
                 + lax.broadcasted_iota(jnp.int32, (TOP_K, tm), 1))
        dest_v[...] = spare
        dest_to_smem()

    def issue_prev_rows(lo, hi):
        for t in range(lo, hi):
            for k in range(TOP_K):
                row_copy(1 - slot, t, k, dest_s[k, t]).start()

    phase_rows = (0, (tm * 7) // 16, (tm * 12) // 16, tm)

    mod = mod_ref[...]
    sh_m, sc_m, g_m = mod[:, 0:d], mod[:, d:2 * d], mod[:, 2 * d:3 * d]
    sh_f, sc_f = mod[:, 3 * d:4 * d], mod[:, 4 * d:5 * d]

    x = x_ref[...]
    gs = gmix_ref[...] * (1.0 + sc_m)
    r = lax.rsqrt(jnp.mean(x * x, axis=-1, keepdims=True) + NORM_EPS)
    hbuf[...] = (x * r * gs + sh_m).astype(jnp.bfloat16)

    h = hbuf[...]
    issue_prev_rows(phase_rows[0], phase_rows[1])
    zbuf[:, 0:3 * d] = dot(h, win_ref[:, 0:3 * d])
    zbuf[:, 3 * d:4 * d] = dot(h, win_ref[:, 5 * d:6 * d])

    cw = conv_ref[...]
    w0, w1, w2 = cw[0:1, :], cw[1:2, :], cw[2:3, :]
    row8 = lax.broadcasted_iota(jnp.int32, (V7X_SUBLANES, d), 0)

    def branch_a(i, carry):
        r0 = pl.multiple_of(i * ROW_CHUNK, ROW_CHUNK)
        rows = pl.ds(r0, ROW_CHUNK)
        p0 = zbuf[rows, 0:d] * zbuf[rows, 2 * d:3 * d]
        rp = pl.ds(pl.multiple_of(jnp.maximum(r0 - V7X_SUBLANES, 0), V7X_SUBLANES), V7X_SUBLANES)
        prev_in = zbuf[rp, 0:d] * zbuf[rp, 2 * d:3 * d]
        prev = jnp.where(i == 0, halo_x[...] * halo_c[...], prev_in)
        r1 = pltpu.roll(p0, 1, 0)
        r2 = pltpu.roll(p0, 2, 0)
        q1 = pltpu.roll(prev, 1, 0)
        q2 = pltpu.roll(prev, 2, 0)
        p1 = jnp.concatenate([jnp.where(row8 < 1, q1, r1[0:V7X_SUBLANES]), r1[V7X_SUBLANES:]], axis=0)
        p2 = jnp.concatenate([jnp.where(row8 < 2, q2, r2[0:V7X_SUBLANES]), r2[V7X_SUBLANES:]], axis=0)
        y_a = zbuf[rows, d:2 * d] * (w0 * p2 + w1 * p1 + w2 * p0)
        mabuf[rows, :] = _sigmoid(zbuf[rows, 3 * d:4 * d]) * y_a
        return carry

    lax.fori_loop(0, n_chunks, branch_a, 0)
    last8 = pl.ds(tm - V7X_SUBLANES, V7X_SUBLANES)
    halo_x[...] = zbuf[last8, 0:d]
    halo_c[...] = zbuf[last8, 2 * d:3 * d]

    issue_prev_rows(phase_rows[1], phase_rows[2])
    zbuf[:, 0:2 * d] = dot(h, win_ref[:, 3 * d:5 * d])
    zbuf[:, 2 * d:3 * d] = dot(h, win_ref[:, 6 * d:7 * d])

    tri = (lax.broadcasted_iota(jnp.int32, (CHUNK, CHUNK), 1)
           <= lax.broadcasted_iota(jnp.int32, (CHUNK, CHUNK), 0))
    wm = [jnp.where(tri, wsp_ref[g], 0.0).astype(jnp.bfloat16) for g in range(GROUPS)]
    gd = d // GROUPS
    lng = lng_ref[...]

    def branch_b(i, carry):
        r0 = pl.multiple_of(i * ROW_CHUNK, ROW_CHUNK)
        rows = pl.ds(r0, ROW_CHUNK)
        gv = _gelu(zbuf[rows, d:2 * d])
        mu = jnp.mean(gv, axis=-1, keepdims=True)
        vc = gv - mu
        var = jnp.mean(vc * vc, axis=-1, keepdims=True)
        vn = (vc * lax.rsqrt(var + NORM_EPS) * lng).astype(jnp.bfloat16)
        s = jnp.concatenate(
            [dot(wm[g], vn[:, g * gd:(g + 1) * gd]) for g in range(GROUPS)], axis=1)
        y_b = _gelu(zbuf[rows, 0:d]) * (s + bsp_ref[...])
        m = mabuf[rows, :] + _sigmoid(zbuf[rows, 2 * d:3 * d]) * y_b
        mbuf[rows, :] = m.astype(jnp.bfloat16)
        return carry

    lax.fori_loop(0, n_chunks, branch_b, 0)

    issue_prev_rows(phase_rows[2], phase_rows[3])
    x1 = x_ref[...] + g_m * dot(mbuf[...], wout_ref[...])
    x1_ref[...] = x1
    gs2 = gffn_ref[...] * (1.0 + sc_f)
    r2n = lax.rsqrt(jnp.mean(x1 * x1, axis=-1, keepdims=True) + NORM_EPS)
    h2 = x1 * r2n * gs2 + sh_f

    @pl.when(step > 0)
    def _():
        tile_rows_wait(slot)

    h2buf[slot] = h2

    nt = (((1,), (1,)), ((), ()))
    dg = functools.partial(lax.dot_general, dimension_numbers=nt,
                           preferred_element_type=jnp.float32)
    h_hi, h_lo = _split_bf16(h2)
    r_hi, r_lo = _split_bf16(rwt_ref[...])
    logits = dg(r_hi, h_hi) + dg(r_hi, h_lo) + dg(r_lo, h_hi) + rb_ref[...]

    iota_e = lax.broadcasted_iota(jnp.int32, (n_exp, tm), 0)
    vals = logits
    tops, idxs, hots = [], [], []
    for _ in range(TOP_K):
        mx = jnp.max(vals, axis=0, keepdims=True)
        ix = jnp.min(jnp.where(vals == mx, iota_e, n_exp), axis=0, keepdims=True)
        hot = iota_e == ix
        vals = jnp.where(hot, -jnp.inf, vals)
        tops.append(mx)
        idxs.append(ix)
        hots.append(hot)
    exps = [jnp.exp(t - tops[0]) for t in tops]
    denom = exps[0] + exps[1] + exps[2] + exps[3]
    for k in range(TOP_K):
        tw_ref[k:k + 1, :] = exps[k] / denom

    member = jnp.where(hots[0] | hots[1] | hots[2] | hots[3], 1.0, 0.0)
    upper = (lax.broadcasted_iota(jnp.int32, (tm, tm), 0)
             < lax.broadcasted_iota(jnp.int32, (tm, tm), 1))
    before = dot(member.astype(jnp.bfloat16), jnp.where(upper, 1.0, 0.0).astype(jnp.bfloat16))
    run = run_ref[...]
    pos = before + run[:, 0:1]
    for k in range(TOP_K):
        rk = jnp.sum(jnp.where(hots[k], pos, 0.0), axis=0, keepdims=True)
        dest_k = idxs[k] * capacity + rk.astype(jnp.int32)
        dest_ref[k:k + 1, :] = dest_k
        dest_v[k:k + 1, :] = dest_k
    run = run + jnp.sum(member, axis=1, keepdims=True)
    run_ref[...] = run
    cnt_ref[...] = run.astype(jnp.int32)

    dest_to_smem()

    @pl.when(step == n_steps - 1)
    def _():
        def issue(t, carry):
            for k in range(TOP_K):
                row_copy(slot, t, k, dest_s[k, t]).start()
            return carry

        lax.fori_loop(0, tm, issue, 0)
        tile_rows_wait(1 - slot)
        tile_rows_wait(slot)


def _mixer(x, mod_l, gmix, win_bf, conv_w, wsp, bsp_full, lng, wout_bf, gffn, rwt, rb):
    bsz, seq, d = x.shape
    n_exp = rwt.shape[0]
    tm = MIX_TILE
    n_tok = bsz * seq
    tiles = seq // tm
    const = pl.Buffered(1)

    def cspec(shape):
        return pl.BlockSpec(shape, lambda b, s: (0,) * len(shape), pipeline_mode=const)

    out_shape = (
        jax.ShapeDtypeStruct((bsz, seq, d), jnp.float32),
        jax.ShapeDtypeStruct((TOP_K, n_tok), jnp.int32),
        jax.ShapeDtypeStruct((TOP_K, n_tok), jnp.float32),
        jax.ShapeDtypeStruct((n_exp, V7X_LANES), jnp.int32),
        jax.ShapeDtypeStruct((n_exp * n_tok + TOP_K * tm, d), jnp.float32),
    )
    tok_t = lambda b, s: (0, b * tiles + s)
    return pl.pallas_call(
        functools.partial(_mixer_kernel, capacity=n_tok),
        grid=(bsz, tiles),
        in_specs=[
            pl.BlockSpec((None, tm, d), lambda b, s: (b, s, 0)),
            pl.BlockSpec((None, 1, 6 * d), lambda b, s: (b, 0, 0)),
            cspec((1, d)),
            cspec(win_bf.shape),
            cspec(conv_w.shape),
            cspec(wsp.shape),
            cspec(bsp_full.shape),
            cspec((1, d)),
            cspec(wout_bf.shape),
            cspec((1, d)),
            cspec(rwt.shape),
            cspec(rb.shape),
        ],
        out_specs=(
            pl.BlockSpec((None, tm, d), lambda b, s: (b, s, 0)),
            pl.BlockSpec((TOP_K, tm), tok_t),
            pl.BlockSpec((TOP_K, tm), tok_t),
            pl.BlockSpec((n_exp, V7X_LANES), lambda b, s: (0, 0)),
            pl.BlockSpec(memory_space=pl.ANY),
        ),
        out_shape=out_shape,
        scratch_shapes=[
            pltpu.VMEM((tm, d), jnp.bfloat16),
            pltpu.VMEM((tm, 4 * d), jnp.float32),
            pltpu.VMEM((V7X_SUBLANES, d), jnp.float32),
            pltpu.VMEM((V7X_SUBLANES, d), jnp.float32),
            pltpu.VMEM((tm, d), jnp.float32),
            pltpu.VMEM((tm, d), jnp.bfloat16),
            pltpu.VMEM((n_exp, V7X_LANES), jnp.float32),
            pltpu.VMEM((2, tm, d), jnp.float32),
            pltpu.VMEM((TOP_K, tm), jnp.int32),
            pltpu.SMEM((TOP_K, tm), jnp.int32),
            pltpu.SemaphoreType.DMA(()),
            pltpu.SemaphoreType.DMA((2,)),
        ],
        compiler_params=pltpu.CompilerParams(
            dimension_semantics=("arbitrary", "arbitrary"),
            vmem_limit_bytes=V7X_VMEM_LIMIT_BYTES),
        name="mixer_router_dispatch",
    )(x, mod_l, gmix, win_bf, conv_w, wsp, bsp_full, lng, wout_bf, gffn, rwt, rb)


def _experts_kernel(blk_ref, exp_ref, nv_ref,
                    xs_ref, wgu_ref, bgu_ref, wdn_ref, bdn_ref, y_ref, wgu_bf, wdn_bf):
    i = pl.program_id(0)
    rows, d = xs_ref.shape
    d_ff = wdn_ref.shape[0]
    prev = jnp.maximum(i - 1, 0)
    new_expert = jnp.logical_or(i == 0, exp_ref[i] != exp_ref[prev])
    n_valid = nv_ref[i]

    @pl.when(new_expert)
    def _():
        wgu_bf[...] = wgu_ref[...].astype(jnp.bfloat16)
        wdn_bf[...] = wdn_ref[...].astype(jnp.bfloat16)

    @pl.when(n_valid > 0)
    def _():
        ridx = lax.broadcasted_iota(jnp.int32, (rows, d), 0)
        xs = jnp.where(ridx < n_valid, xs_ref[...], 0.0).astype(jnp.bfloat16)
        gu = jnp.dot(xs, wgu_bf[...], preferred_element_type=jnp.float32) + bgu_ref[...]
        gate = jnp.minimum(gu[:, :d_ff], SWIGLU_LIMIT)
        up = jnp.clip(gu[:, d_ff:], -SWIGLU_LIMIT, SWIGLU_LIMIT)
        act = gate * _sigmoid(SWIGLU_ALPHA * gate) * (up + 1.0)
        y_ref[...] = jnp.dot(act.astype(jnp.bfloat16), wdn_bf[...],
                             preferred_element_type=jnp.float32) + bdn_ref[...]


def _experts(items, xs, w_gu, b_gu, w_dn, b_dn, layer):
    n_rows, d = xs.shape
    blk, exp, n_valid = items
    n_items = blk.shape[0]
    rows = EXPERT_BLOCK
    d_ff2 = w_gu.shape[-1]
    d_ff = w_dn.shape[-2]
    n_exp = w_gu.shape[1]
    depth = w_gu.shape[0]
    wmap = lambda i, blk, exp, nv: (layer, exp[i], 0, 0)
    grid_spec = pltpu.PrefetchScalarGridSpec(
        num_scalar_prefetch=3,
        grid=(n_items,),
        in_specs=[
            pl.BlockSpec((rows, d), lambda i, blk, exp, nv: (blk[i], 0)),
            pl.BlockSpec((None, None, d, d_ff2), wmap),
            pl.BlockSpec((None, None, 1, d_ff2), wmap),
            pl.BlockSpec((None, None, d_ff, d), wmap),
            pl.BlockSpec((None, None, 1, d), wmap),
        ],
        out_specs=pl.BlockSpec((rows, d), lambda i, blk, exp, nv: (blk[i], 0)),
        scratch_shapes=[
            pltpu.VMEM((d, d_ff2), jnp.bfloat16),
            pltpu.VMEM((d_ff, d), jnp.bfloat16),
        ],
    )
    return pl.pallas_call(
        _experts_kernel,
        grid_spec=grid_spec,
        out_shape=jax.ShapeDtypeStruct((n_rows, d), jnp.float32),
        compiler_params=pltpu.CompilerParams(
            dimension_semantics=("arbitrary",),
            vmem_limit_bytes=V7X_VMEM_LIMIT_BYTES),
        name="experts",
    )(blk, exp, n_valid, xs, w_gu, b_gu.reshape(depth, n_exp, 1, d_ff2), w_dn,
      b_dn.reshape(depth, n_exp, 1, d))


def _expert_items(counts, capacity, n_assign):
    n_exp = counts.shape[0]
    rows = EXPERT_BLOCK
    n_items = n_assign // rows + n_exp
    per_exp = (counts + rows - 1) // rows
    item_end = jnp.cumsum(per_exp)
    item_start = item_end - per_exp
    total = item_end[-1]
    idx = jnp.arange(n_items, dtype=jnp.int32)
    pick = jnp.minimum(idx, total - 1)
    exp = jnp.sum((item_end[None, :] <= pick[:, None]).astype(jnp.int32), axis=1)
    sub = pick - item_start[exp]
    blk = exp * (capacity // rows) + sub
    n_valid = jnp.where(idx < total, jnp.minimum(counts[exp] - sub * rows, rows), 0)
    return blk.astype(jnp.int32), exp.astype(jnp.int32), n_valid.astype(jnp.int32)


def _combine_kernel(dest_ref, x1_ref, w_ref, mod_ref, fin_ref, y_ref, o_ref, gbuf, sem,
                    *, final_norm):
    tmd, d = x1_ref.shape

    def row_copy(t, k):
        src = dest_ref[t * TOP_K + k]
        return pltpu.make_async_copy(y_ref.at[pl.ds(src, 1)], gbuf.at[k, pl.ds(t, 1)], sem)

    def issue(t, carry):
        for k in range(TOP_K):
            row_copy(t, k).start()
        return carry

    lax.fori_loop(0, tmd, issue, 0)
    for k in range(TOP_K):
        pltpu.make_async_copy(y_ref.at[pl.ds(0, tmd)], gbuf.at[k], sem).wait()

    w = w_ref[...]
    acc = w[:, 0:1] * gbuf[0]
    for k in range(1, TOP_K):
        acc = acc + w[:, k:k + 1] * gbuf[k]
    g_f = mod_ref[:, 5 * d:6 * d]
    x2 = x1_ref[...] + g_f * acc
    if final_norm:
        r = lax.rsqrt(jnp.mean(x2 * x2, axis=-1, keepdims=True) + NORM_EPS)
        x2 = x2 * r * fin_ref[...]
    o_ref[...] = x2


def _combine(dest_flat, x1, w_tok, mod_l, fin_g, y, final_norm):
    bsz, seq, d = x1.shape
    tmd = COMBINE_TILE
    tiles = seq // tmd
    return pl.pallas_call(
        functools.partial(_combine_kernel, final_norm=final_norm),
        grid=(bsz, tiles),
        in_specs=[
            pl.BlockSpec((tmd * TOP_K,), lambda b, s: (b * tiles + s,), memory_space=pltpu.SMEM),
            pl.BlockSpec((None, tmd, d), lambda b, s: (b, s, 0)),
            pl.BlockSpec((tmd, TOP_K), lambda b, s: (b * tiles + s, 0)),
            pl.BlockSpec((None, 1, 6 * d), lambda b, s: (b, 0, 0)),
            pl.BlockSpec((1, d), lambda b, s: (0, 0)),
            pl.BlockSpec(memory_space=pl.ANY),
        ],
        out_specs=pl.BlockSpec((None, tmd, d), lambda b, s: (b, s, 0)),
        out_shape=jax.ShapeDtypeStruct((bsz, seq, d), jnp.float32),
        scratch_shapes=[
            pltpu.VMEM((TOP_K, tmd, d), jnp.float32),
            pltpu.SemaphoreType.DMA(()),
        ],
        compiler_params=pltpu.CompilerParams(dimension_semantics=("arbitrary", "arbitrary")),
        name="combine",
    )(dest_flat, x1, w_tok, mod_l, fin_g, y)


def kernel(x, c, w_ada, b_ada, norm_mix_g, w_in, conv_w, w_spatial, b_spatial, ln_v_g, w_out,
           norm_ffn_g, router_w, router_b, w_gate_up, b_gate_up, w_down, b_down, final_norm_g):
    bsz, seq, d = x.shape
    depth = w_ada.shape[0]
    n_exp = router_w.shape[-1]
    n_tok = bsz * seq
    assert seq % MIX_TILE == 0 and MIX_TILE % CHUNK == 0
    assert seq % COMBINE_TILE == 0 and n_tok % EXPERT_BLOCK == 0

    mod = _modulation(c, w_ada, b_ada)
    win_bf = w_in.astype(jnp.bfloat16)
    wout_bf = w_out.astype(jnp.bfloat16)
    bsp_full = jnp.repeat(jnp.swapaxes(b_spatial, 1, 2), d // GROUPS, axis=2)
    rwt = jnp.swapaxes(router_w, 1, 2)
    fin_g = final_norm_g.reshape(1, d)

    for l in range(depth):
        x1, dest, top_w, cnt, xs = _mixer(
            x, mod[l], norm_mix_g[l].reshape(1, d), win_bf[l], conv_w[l], w_spatial[l],
            bsp_full[l], ln_v_g[l].reshape(1, d), wout_bf[l], norm_ffn_g[l].reshape(1, d),
            rwt[l], router_b[l].reshape(n_exp, 1))
        items = _expert_items(cnt[:, 0], n_tok, n_tok * TOP_K)
        y = _experts(items, xs, w_gate_up, b_gate_up, w_down, b_down, l)
        x = _combine(dest.T.reshape(-1), x1, top_w.T, mod[l], fin_g, y,
                     final_norm=(l == depth - 1))
    return x
```

```python
import functools
import math

import jax
import jax.numpy as jnp
from jax import lax
from jax.experimental import pallas as pl
from jax.experimental.pallas import tpu as pltpu

CONV_K = 3
GROUPS = 8
CHUNK = 128
TOP_K = 4
SWIGLU_LIMIT = 7.0
SWIGLU_ALPHA = 1.702
NORM_EPS = 1e-6

V7X_SUBLANES = 8
V7X_LANES = 128
V7X_VMEM_LIMIT_BYTES = 56 * 1024 * 1024

MIX_TILE = 512
ROW_CHUNK = CHUNK
COMBINE_TILE = 256
EXPERT_BLOCK = 512


def _split_bf16(a):
    hi = a.astype(jnp.bfloat16)
    lo = (a - hi.astype(jnp.float32)).astype(jnp.bfloat16)
    return hi, lo


def _gelu(a):
    return 0.5 * a * (1.0 + lax.erf(a * (1.0 / math.sqrt(2.0))))


def _sigmoid(a):
    return 1.0 / (1.0 + jnp.exp(-a))


def _pack_rows(a):
    half = a.shape[1] // 2

    def bits(v):
        return lax.bitcast_convert_type(v.astype(jnp.bfloat16).astype(jnp.float32), jnp.uint32)

    return bits(a[:, :half]) | (bits(a[:, half:]) >> 16)


def _unpack_rows(p):
    hi = lax.bitcast_convert_type(p & jnp.uint32(0xFFFF0000), jnp.float32)
    lo = lax.bitcast_convert_type(p << 16, jnp.float32)
    return hi, lo


def _mod_kernel(c_ref, w_ref, b_ref, o_ref):
    c = c_ref[...]
    s = c * _sigmoid(c)
    s_hi, s_lo = _split_bf16(s)
    w_hi, w_lo = _split_bf16(w_ref[...])
    dot = functools.partial(jnp.dot, preferred_element_type=jnp.float32)
    o_ref[...] = dot(s_hi, w_hi) + dot(s_lo, w_hi) + dot(s_hi, w_lo) + b_ref[...]


def _modulation(c, w_ada, b_ada):
    depth, d, d6 = w_ada.shape
    bsz = c.shape[0]
    rows = -(-bsz // V7X_SUBLANES) * V7X_SUBLANES
    c_pad = jnp.zeros((rows, d), jnp.float32).at[:bsz].set(c)
    n_col = d6 // d
    out = pl.pallas_call(
        _mod_kernel,
        grid=(depth, n_col),
        in_specs=[
            pl.BlockSpec((rows, d), lambda l, j: (0, 0)),
            pl.BlockSpec((None, d, d), lambda l, j: (l, 0, j)),
            pl.BlockSpec((None, 1, d), lambda l, j: (l, 0, j)),
        ],
        out_specs=pl.BlockSpec((None, rows, d), lambda l, j: (l, 0, j)),
        out_shape=jax.ShapeDtypeStruct((depth, rows, d6), jnp.float32),
        name="adaln_modulation",
    )(c_pad, w_ada, b_ada.reshape(depth, 1, d6))
    return out[:, :bsz].reshape(depth, bsz, 1, d6)


def _mixer_kernel(*refs, capacity, fused):
    if fused:
        (x_ref, mod_ref, gmix_ref, win_ref, conv_ref, wsp_ref, bsp_ref, lng_ref, wout_ref,
         gffn_ref, rwt_ref, rb_ref, pdest_cur, pdest_nxt, pw_ref, pmod_ref, y_ref,
         x1_ref, dest_ref, tw_ref, cnt_ref, xs_ref,
         hbuf, zbuf, halo_x, halo_c, mabuf, mbuf, run_ref, h2buf, dest_v, dest_s,
         dest_sem, row_sem, xbuf, gbuf, gat_sem) = refs
    else:
        (x_ref, mod_ref, gmix_ref, win_ref, conv_ref, wsp_ref, bsp_ref, lng_ref, wout_ref,
         gffn_ref, rwt_ref, rb_ref,
         x1_ref, dest_ref, tw_ref, cnt_ref, xs_ref,
         hbuf, zbuf, halo_x, halo_c, mabuf, mbuf, run_ref, h2buf, dest_v, dest_s,
         dest_sem, row_sem) = refs
    tm, d = x_ref.shape
    half = d // 2
    n_exp = rwt_ref.shape[0]
    n_chunks = tm // ROW_CHUNK
    step = pl.program_id(0) * pl.num_programs(1) + pl.program_id(1)
    n_steps = pl.num_programs(0) * pl.num_programs(1)
    slot = step % 2
    dot = functools.partial(jnp.dot, preferred_element_type=jnp.float32)

    @pl.when(pl.program_id(1) == 0)
    def _():
        halo_x[...] = jnp.zeros_like(halo_x)
        halo_c[...] = jnp.zeros_like(halo_c)

    def row_copy(which, t, k, row):
        return pltpu.make_async_copy(h2buf.at[which, pl.ds(t, 1)], xs_ref.at[pl.ds(row, 1)],
                                     row_sem.at[which])

    def tile_rows_wait(which):
        for _ in range(TOP_K):
            pltpu.make_async_copy(h2buf.at[which], xs_ref.at[pl.ds(0, tm)],
                                  row_sem.at[which]).wait()

    def dest_to_smem():
        dest_copy = pltpu.make_async_copy(dest_v, dest_s, dest_sem)
        dest_copy.start()
        dest_copy.wait()

    def gather_copy(which, t, k, row):
        return pltpu.make_async_copy(y_ref.at[pl.ds(row, 1)], gbuf.at[which, k, pl.ds(t, 1)],
                                     gat_sem.at[which])

    def tile_gather_wait(which):
        for k in range(TOP_K):
            pltpu.make_async_copy(y_ref.at[pl.ds(0, tm)], gbuf.at[which, k],
                                  gat_sem.at[which]).wait()

    @pl.when(step == 0)
    def _():
        run_ref[...] = jnp.zeros_like(run_ref)
        h2buf[1] = jnp.zeros((tm, half), jnp.uint32)
        spare = (n_exp * capacity + lax.broadcasted_iota(jnp.int32, (TOP_K, tm), 0) * tm
                 + lax.broadcasted_iota(jnp.int32, (TOP_K, tm), 1))
        dest_v[...] = spare
        dest_to_smem()
        if fused:
            def first_gather(t, carry):
                for k in range(TOP_K):
                    gather_copy(0, t, k, pdest_cur[k * tm + t]).start()
                return carry

            lax.fori_loop(0, tm, first_gather, 0)

    def issue_rows(lo, hi):
        for t in range(lo, hi):
            for k in range(TOP_K):
                row_copy(1 - slot, t, k, dest_s[k, t]).start()
                if fused:
                    gather_copy(1 - slot, t, k, pdest_nxt[k * tm + t]).start()

    phase_rows = (0, (tm * 7) // 16, (tm * 12) // 16, tm)

    mod = mod_ref[...]
    sh_m, sc_m, g_m = mod[:, 0:d], mod[:, d:2 * d], mod[:, 2 * d:3 * d]
    sh_f, sc_f = mod[:, 3 * d:4 * d], mod[:, 4 * d:5 * d]

    if fused:
        tile_gather_wait(slot)
        w = pw_ref[...]
        acc_hi, acc_lo = None, None
        for k in range(TOP_K):
            y_hi, y_lo = _unpack_rows(gbuf[slot, k])
            wk = w[:, k:k + 1]
            acc_hi = wk * y_hi if acc_hi is None else acc_hi + wk * y_hi
            acc_lo = wk * y_lo if acc_lo is None else acc_lo + wk * y_lo
        g_prev = pmod_ref[:, 5 * d:6 * d]
        xbuf[:, 0:half] = x_ref[:, 0:half] + g_prev[:, 0:half] * acc_hi
        xbuf[:, half:d] = x_ref[:, half:d] + g_prev[:, half:d] * acc_lo
        xin = xbuf
    else:
        xin = x_ref

    x = xin[...]
    gs = gmix_ref[...] * (1.0 + sc_m)
    r = lax.rsqrt(jnp.mean(x * x, axis=-1, keepdims=True) + NORM_EPS)
    hbuf[...] = (x * r * gs + sh_m).astype(jnp.bfloat16)

    h = hbuf[...]
    issue_rows(phase_rows[0], phase_rows[1])
    zbuf[:, 0:3 * d] = dot(h, win_ref[:, 0:3 * d])
    zbuf[:, 3 * d:4 * d] = dot(h, win_ref[:, 5 * d:6 * d])

    cw = conv_ref[...]
    w0, w1, w2 = cw[0:1, :], cw[1:2, :], cw[2:3, :]
    row8 = lax.broadcasted_iota(jnp.int32, (V7X_SUBLANES, d), 0)

    def branch_a(i, carry):
        r0 = pl.multiple_of(i * ROW_CHUNK, ROW_CHUNK)
        rows = pl.ds(r0, ROW_CHUNK)
        p0 = zbuf[rows, 0:d] * zbuf[rows, 2 * d:3 * d]
        rp = pl.ds(pl.multiple_of(jnp.maximum(r0 - V7X_SUBLANES, 0), V7X_SUBLANES), V7X_SUBLANES)
        prev_in = zbuf[rp, 0:d] * zbuf[rp, 2 * d:3 * d]
        prev = jnp.where(i == 0, halo_x[...] * halo_c[...], prev_in)
        r1 = pltpu.roll(p0, 1, 0)
        r2 = pltpu.roll(p0, 2, 0)
        q1 = pltpu.roll(prev, 1, 0)
        q2 = pltpu.roll(prev, 2, 0)
        p1 = jnp.concatenate([jnp.where(row8 < 1, q1, r1[0:V7X_SUBLANES]), r1[V7X_SUBLANES:]], axis=0)
        p2 = jnp.concatenate([jnp.where(row8 < 2, q2, r2[0:V7X_SUBLANES]), r2[V7X_SUBLANES:]], axis=0)
        y_a = zbuf[rows, d:2 * d] * (w0 * p2 + w1 * p1 + w2 * p0)
        mabuf[rows, :] = _sigmoid(zbuf[rows, 3 * d:4 * d]) * y_a
        return carry

    lax.fori_loop(0, n_chunks, branch_a, 0)
    last8 = pl.ds(tm - V7X_SUBLANES, V7X_SUBLANES)
    halo_x[...] = zbuf[last8, 0:d]
    halo_c[...] = zbuf[last8, 2 * d:3 * d]

    issue_rows(phase_rows[1], phase_rows[2])
    zbuf[:, 0:2 * d] = dot(h, win_ref[:, 3 * d:5 * d])
    zbuf[:, 2 * d:3 * d] = dot(h, win_ref[:, 6 * d:7 * d])

    tri = (lax.broadcasted_iota(jnp.int32, (CHUNK, CHUNK), 1)
           <= lax.broadcasted_iota(jnp.int32, (CHUNK, CHUNK), 0))
    wm = [jnp.where(tri, wsp_ref[g], 0.0).astype(jnp.bfloat16) for g in range(GROUPS)]
    gd = d // GROUPS
    lng = lng_ref[...]

    def branch_b(i, carry):
        r0 = pl.multiple_of(i * ROW_CHUNK, ROW_CHUNK)
        rows = pl.ds(r0, ROW_CHUNK)
        gv = _gelu(zbuf[rows, d:2 * d])
        mu = jnp.mean(gv, axis=-1, keepdims=True)
        vc = gv - mu
        var = jnp.mean(vc * vc, axis=-1, keepdims=True)
        vn = (vc * lax.rsqrt(var + NORM_EPS) * lng).astype(jnp.bfloat16)
        s = jnp.concatenate(
            [dot(wm[g], vn[:, g * gd:(g + 1) * gd]) for g in range(GROUPS)], axis=1)
        y_b = _gelu(zbuf[rows, 0:d]) * (s + bsp_ref[...])
        m = mabuf[rows, :] + _sigmoid(zbuf[rows, 2 * d:3 * d]) * y_b
        mbuf[rows, :] = m.astype(jnp.bfloat16)
        return carry

    lax.fori_loop(0, n_chunks, branch_b, 0)

    issue_rows(phase_rows[2], phase_rows[3])
    x1 = xin[...] + g_m * dot(mbuf[...], wout_ref[...])
    x1_ref[...] = x1
    gs2 = gffn_ref[...] * (1.0 + sc_f)
    r2n = lax.rsqrt(jnp.mean(x1 * x1, axis=-1, keepdims=True) + NORM_EPS)
    h2 = x1 * r2n * gs2 + sh_f

    @pl.when(step > 0)
    def _():
        tile_rows_wait(slot)

    h2buf[slot] = _pack_rows(h2)

    nt = (((1,), (1,)), ((), ()))
    dg = functools.partial(lax.dot_general, dimension_numbers=nt,
                           preferred_element_type=jnp.float32)
    h_hi, h_lo = _split_bf16(h2)
    r_hi, r_lo = _split_bf16(rwt_ref[...])
    logits = dg(r_hi, h_hi) + dg(r_hi, h_lo) + dg(r_lo, h_hi) + rb_ref[...]

    iota_e = lax.broadcasted_iota(jnp.int32, (n_exp, tm), 0)
    vals = logits
    tops, idxs, hots = [], [], []
    for _ in range(TOP_K):
        mx = jnp.max(vals, axis=0, keepdims=True)
        ix = jnp.min(jnp.where(vals == mx, iota_e, n_exp), axis=0, keepdims=True)
        hot = iota_e == ix
        vals = jnp.where(hot, -jnp.inf, vals)
        tops.append(mx)
        idxs.append(ix)
        hots.append(hot)
    exps = [jnp.exp(t - tops[0]) for t in tops]
    denom = exps[0] + exps[1] + exps[2] + exps[3]
    for k in range(TOP_K):
        tw_ref[k:k + 1, :] = exps[k] / denom

    member = jnp.where(hots[0] | hots[1] | hots[2] | hots[3], 1.0, 0.0)
    upper = (lax.broadcasted_iota(jnp.int32, (tm, tm), 0)
             < lax.broadcasted_iota(jnp.int32, (tm, tm), 1))
    before = dot(member.astype(jnp.bfloat16), jnp.where(upper, 1.0, 0.0).astype(jnp.bfloat16))
    run = run_ref[...]
    pos = before + run[:, 0:1]
    for k in range(TOP_K):
        rk = jnp.sum(jnp.where(hots[k], pos, 0.0), axis=0, keepdims=True)
        dest_k = idxs[k] * capacity + rk.astype(jnp.int32)
        dest_ref[k:k + 1, :] = dest_k
        dest_v[k:k + 1, :] = dest_k
    run = run + jnp.sum(member, axis=1, keepdims=True)
    run_ref[...] = run
    cnt_ref[...] = run.astype(jnp.int32)

    dest_to_smem()

    @pl.when(step == n_steps - 1)
    def _():
        def issue(t, carry):
            for k in range(TOP_K):
                row_copy(slot, t, k, dest_s[k, t]).start()
            return carry

        lax.fori_loop(0, tm, issue, 0)
        tile_rows_wait(1 - slot)
        tile_rows_wait(slot)
        if fused:
            tile_gather_wait(1 - slot)


def _tile_major(a, tm):
    n_tok = a.shape[1]
    return a.reshape(TOP_K, n_tok // tm, tm).transpose(1, 0, 2).reshape(-1)


def _mixer(x, mod_l, gmix, win_bf, conv_w, wsp, bsp_full, lng, wout_bf, gffn, rwt, rb, prev):
    bsz, seq, d = x.shape
    n_exp = rwt.shape[0]
    tm = MIX_TILE
    n_tok = bsz * seq
    tiles = seq // tm
    n_steps = bsz * tiles
    fused = prev is not None
    const = pl.Buffered(1)

    def cspec(shape):
        return pl.BlockSpec(shape, lambda b, s: (0,) * len(shape), pipeline_mode=const)

    out_shape = (
        jax.ShapeDtypeStruct((bsz, seq, d), jnp.float32),
        jax.ShapeDtypeStruct((TOP_K, n_tok), jnp.int32),
        jax.ShapeDtypeStruct((TOP_K, n_tok), jnp.float32),
        jax.ShapeDtypeStruct((n_exp, V7X_LANES), jnp.int32),
        jax.ShapeDtypeStruct((n_exp * n_tok + TOP_K * tm, d // 2), jnp.uint32),
    )
    tok_t = lambda b, s: (0, b * tiles + s)
    in_specs = [
        pl.BlockSpec((None, tm, d), lambda b, s: (b, s, 0)),
        pl.BlockSpec((None, 1, 6 * d), lambda b, s: (b, 0, 0)),
        cspec((1, d)),
        cspec(win_bf.shape),
        cspec(conv_w.shape),
        cspec(wsp.shape),
        cspec(bsp_full.shape),
        cspec((1, d)),
        cspec(wout_bf.shape),
        cspec((1, d)),
        cspec(rwt.shape),
        cspec(rb.shape),
    ]
    args = [x, mod_l, gmix, win_bf, conv_w, wsp, bsp_full, lng, wout_bf, gffn, rwt, rb]
    scratch = [
        pltpu.VMEM((tm, d), jnp.bfloat16),
        pltpu.VMEM((tm, 4 * d), jnp.float32),
        pltpu.VMEM((V7X_SUBLANES, d), jnp.float32),
        pltpu.VMEM((V7X_SUBLANES, d), jnp.float32),
        pltpu.VMEM((tm, d), jnp.float32),
        pltpu.VMEM((tm, d), jnp.bfloat16),
        pltpu.VMEM((n_exp, V7X_LANES), jnp.float32),
        pltpu.VMEM((2, tm, d // 2), jnp.uint32),
        pltpu.VMEM((TOP_K, tm), jnp.int32),
        pltpu.SMEM((TOP_K, tm), jnp.int32),
        pltpu.SemaphoreType.DMA(()),
        pltpu.SemaphoreType.DMA((2,)),
    ]
    if fused:
        pdest, pw, pmod, y = prev
        pdest_flat = _tile_major(pdest, tm)
        in_specs += [
            pl.BlockSpec((TOP_K * tm,), lambda b, s: (b * tiles + s,), memory_space=pltpu.SMEM),
            pl.BlockSpec((TOP_K * tm,),
                         lambda b, s: (jnp.minimum(b * tiles + s + 1, n_steps - 1),),
                         memory_space=pltpu.SMEM),
            pl.BlockSpec((tm, TOP_K), lambda b, s: (b * tiles + s, 0)),
            pl.BlockSpec((None, 1, 6 * d), lambda b, s: (b, 0, 0)),
            pl.BlockSpec(memory_space=pl.ANY),
        ]
        args += [pdest_flat, pdest_flat, pw.T, pmod, y]
        scratch += [
            pltpu.VMEM((tm, d), jnp.float32),
            pltpu.VMEM((2, TOP_K, tm, d // 2), jnp.uint32),
            pltpu.SemaphoreType.DMA((2,)),
        ]
    return pl.pallas_call(
        functools.partial(_mixer_kernel, capacity=n_tok, fused=fused),
        grid=(bsz, tiles),
        in_specs=in_specs,
        out_specs=(
            pl.BlockSpec((None, tm, d), lambda b, s: (b, s, 0)),
            pl.BlockSpec((TOP_K, tm), tok_t),
            pl.BlockSpec((TOP_K, tm), tok_t),
            pl.BlockSpec((n_exp, V7X_LANES), lambda b, s: (0, 0)),
            pl.BlockSpec(memory_space=pl.ANY),
        ),
        out_shape=out_shape,
        scratch_shapes=scratch,
        compiler_params=pltpu.CompilerParams(
            dimension_semantics=("arbitrary", "arbitrary"),
            vmem_limit_bytes=V7X_VMEM_LIMIT_BYTES),
        name="combine_mixer_router_dispatch" if fused else "mixer_router_dispatch",
    )(*args)


def _experts_kernel(blk_ref, exp_ref, nv_ref,
                    xs_ref, wgu_ref, bgu_ref, wdn_ref, bdn_ref, y_ref, wgu_bf, wdn_bf):
    i = pl.program_id(0)
    rows, half = xs_ref.shape
    d_ff = wdn_ref.shape[0]
    prev = jnp.maximum(i - 1, 0)
    new_expert = jnp.logical_or(i == 0, exp_ref[i] != exp_ref[prev])
    n_valid = nv_ref[i]

    @pl.when(new_expert)
    def _():
        wgu_bf[...] = wgu_ref[...].astype(jnp.bfloat16)
        wdn_bf[...] = wdn_ref[...].astype(jnp.bfloat16)

    @pl.when(n_valid > 0)
    def _():
        ridx = lax.broadcasted_iota(jnp.int32, (rows, half), 0)
        packed = jnp.where(ridx < n_valid, xs_ref[...], jnp.uint32(0))
        x_hi, x_lo = _unpack_rows(packed)
        xs = jnp.concatenate([x_hi.astype(jnp.bfloat16), x_lo.astype(jnp.bfloat16)], axis=1)
        gu = jnp.dot(xs, wgu_bf[...], preferred_element_type=jnp.float32) + bgu_ref[...]
        gate = jnp.minimum(gu[:, :d_ff], SWIGLU_LIMIT)
        up = jnp.clip(gu[:, d_ff:], -SWIGLU_LIMIT, SWIGLU_LIMIT)
        act = gate * _sigmoid(SWIGLU_ALPHA * gate) * (up + 1.0)
        y = jnp.dot(act.astype(jnp.bfloat16), wdn_bf[...],
                    preferred_element_type=jnp.float32) + bdn_ref[...]
        y_ref[...] = _pack_rows(y)


def _experts(items, xs, w_gu, b_gu, w_dn, b_dn, layer):
    n_rows, half = xs.shape
    d = 2 * half
    blk, exp, n_valid = items
    n_items = blk.shape[0]
    rows = EXPERT_BLOCK
    d_ff2 = w_gu.shape[-1]
    d_ff = w_dn.shape[-2]
    n_exp = w_gu.shape[1]
    depth = w_gu.shape[0]
    wmap = lambda i, blk, exp, nv: (layer, exp[i], 0, 0)
    grid_spec = pltpu.PrefetchScalarGridSpec(
        num_scalar_prefetch=3,
        grid=(n_items,),
        in_specs=[
            pl.BlockSpec((rows, half), lambda i, blk, exp, nv: (blk[i], 0)),
            pl.BlockSpec((None, None, d, d_ff2), wmap),
            pl.BlockSpec((None, None, 1, d_ff2), wmap),
            pl.BlockSpec((None, None, d_ff, d), wmap),
            pl.BlockSpec((None, None, 1, d), wmap),
        ],
        out_specs=pl.BlockSpec((rows, half), lambda i, blk, exp, nv: (blk[i], 0)),
        scratch_shapes=[
            pltpu.VMEM((d, d_ff2), jnp.bfloat16),
            pltpu.VMEM((d_ff, d), jnp.bfloat16),
        ],
    )
    return pl.pallas_call(
        _experts_kernel,
        grid_spec=grid_spec,
        out_shape=jax.ShapeDtypeStruct((n_rows, half), jnp.uint32),
        compiler_params=pltpu.CompilerParams(
            dimension_semantics=("arbitrary",),
            vmem_limit_bytes=V7X_VMEM_LIMIT_BYTES),
        name="experts",
    )(blk, exp, n_valid, xs, w_gu, b_gu.reshape(depth, n_exp, 1, d_ff2), w_dn,
      b_dn.reshape(depth, n_exp, 1, d))


def _expert_items(counts, capacity, n_assign):
    n_exp = counts.shape[0]
    rows = EXPERT_BLOCK
    n_items = n_assign // rows + n_exp
    per_exp = (counts + rows - 1) // rows
    item_end = jnp.cumsum(per_exp)
    item_start = item_end - per_exp
    total = item_end[-1]
    idx = jnp.arange(n_items, dtype=jnp.int32)
    pick = jnp.minimum(idx, total - 1)
    exp = jnp.sum((item_end[None, :] <= pick[:, None]).astype(jnp.int32), axis=1)
    sub = pick - item_start[exp]
    blk = exp * (capacity // rows) + sub
    n_valid = jnp.where(idx < total, jnp.minimum(counts[exp] - sub * rows, rows), 0)
    return blk.astype(jnp.int32), exp.astype(jnp.int32), n_valid.astype(jnp.int32)


def _combine_kernel(dest_ref, x1_ref, w_ref, mod_ref, fin_ref, y_ref, o_ref, gbuf, sem):
    tmd, d = x1_ref.shape
    half = d // 2

    def issue(t, carry):
        for k in range(TOP_K):
            pltpu.make_async_copy(y_ref.at[pl.ds(dest_ref[k * tmd + t], 1)],
                                  gbuf.at[k, pl.ds(t, 1)], sem).start()
        return carry

    lax.fori_loop(0, tmd, issue, 0)
    for k in range(TOP_K):
        pltpu.make_async_copy(y_ref.at[pl.ds(0, tmd)], gbuf.at[k], sem).wait()

    w = w_ref[...]
    acc_hi, acc_lo = None, None
    for k in range(TOP_K):
        y_hi, y_lo = _unpack_rows(gbuf[k])
        wk = w[:, k:k + 1]
        acc_hi = wk * y_hi if acc_hi is None else acc_hi + wk * y_hi
        acc_lo = wk * y_lo if acc_lo is None else acc_lo + wk * y_lo
    g_f = mod_ref[:, 5 * d:6 * d]
    x_hi = x1_ref[:, 0:half] + g_f[:, 0:half] * acc_hi
    x_lo = x1_ref[:, half:d] + g_f[:, half:d] * acc_lo
    ms = (jnp.sum(x_hi * x_hi, axis=-1, keepdims=True)
          + jnp.sum(x_lo * x_lo, axis=-1, keepdims=True)) * (1.0 / d)
    r = lax.rsqrt(ms + NORM_EPS)
    o_ref[:, 0:half] = x_hi * r * fin_ref[:, 0:half]
    o_ref[:, half:d] = x_lo * r * fin_ref[:, half:d]


def _combine(dest, x1, top_w, mod_l, fin_g, y):
    bsz, seq, d = x1.shape
    tmd = COMBINE_TILE
    tiles = seq // tmd
    return pl.pallas_call(
        _combine_kernel,
        grid=(bsz, tiles),
        in_specs=[
            pl.BlockSpec((tmd * TOP_K,), lambda b, s: (b * tiles + s,), memory_space=pltpu.SMEM),
            pl.BlockSpec((None, tmd, d), lambda b, s: (b, s, 0)),
            pl.BlockSpec((tmd, TOP_K), lambda b, s: (b * tiles + s, 0)),
            pl.BlockSpec((None, 1, 6 * d), lambda b, s: (b, 0, 0)),
            pl.BlockSpec((1, d), lambda b, s: (0, 0)),
            pl.BlockSpec(memory_space=pl.ANY),
        ],
        out_specs=pl.BlockSpec((None, tmd, d), lambda b, s: (b, s, 0)),
        out_shape=jax.ShapeDtypeStruct((bsz, seq, d), jnp.float32),
        scratch_shapes=[
            pltpu.VMEM((TOP_K, tmd, d // 2), jnp.uint32),
            pltpu.SemaphoreType.DMA(()),
        ],
        compiler_params=pltpu.CompilerParams(dimension_semantics=("arbitrary", "arbitrary")),
        name="final_combine",
    )(_tile_major(dest, tmd), x1, top_w.T, mod_l, fin_g, y)


def kernel(x, c, w_ada, b_ada, norm_mix_g, w_in, conv_w, w_spatial, b_spatial, ln_v_g, w_out,
           norm_ffn_g, router_w, router_b, w_gate_up, b_gate_up, w_down, b_down, final_norm_g):
    bsz, seq, d = x.shape
    depth = w_ada.shape[0]
    n_exp = router_w.shape[-1]
    n_tok = bsz * seq
    assert seq % MIX_TILE == 0 and MIX_TILE % CHUNK == 0
    assert seq % COMBINE_TILE == 0 and n_tok % EXPERT_BLOCK == 0

    mod = _modulation(c, w_ada, b_ada)
    win_bf = w_in.astype(jnp.bfloat16)
    wout_bf = w_out.astype(jnp.bfloat16)
    bsp_full = jnp.repeat(jnp.swapaxes(b_spatial, 1, 2), d // GROUPS, axis=2)
    rwt = jnp.swapaxes(router_w, 1, 2)
    fin_g = final_norm_g.reshape(1, d)

    prev = None
    for l in range(depth):
        x, dest, top_w, cnt, xs = _mixer(
            x, mod[l], norm_mix_g[l].reshape(1, d), win_bf[l], conv_w[l], w_spatial[l],
            bsp_full[l], ln_v_g[l].reshape(1, d), wout_bf[l], norm_ffn_g[l].reshape(1, d),
            rwt[l], router_b[l].reshape(n_exp, 1), prev)
        items = _expert_items(cnt[:, 0], n_tok, n_tok * TOP_K)
        y = _experts(items, xs, w_gate_up, b_gate_up, w_down, b_down, l)
        prev = (dest, top_w, mod[l], y)
    return _combine(dest, x, top_w, mod[depth - 1], fin_g, y)
```

```python
import functools
import math

import jax
import jax.numpy as jnp
from jax import lax
from jax.experimental import pallas as pl
from jax.experimental.pallas import tpu as pltpu

CONV_K = 3
GROUPS = 8
CHUNK = 128
TOP_K = 4
SWIGLU_LIMIT = 7.0
SWIGLU_ALPHA = 1.702
NORM_EPS = 1e-6

V7X_SUBLANES = 8
V7X_LANES = 128
V7X_VMEM_LIMIT_BYTES = 56 * 1024 * 1024
MXU_COLS = 256

MIX_TILE = 512
ROW_CHUNK = CHUNK
EXPERT_BLOCK = 512
TAG = V7X_LANES


def _split_bf16(a):
    hi = a.astype(jnp.bfloat16)
    lo = (a - hi.astype(jnp.float32)).astype(jnp.bfloat16)
    return hi, lo


def _gelu(a):
    return 0.5 * a * (1.0 + lax.erf(a * (1.0 / math.sqrt(2.0))))


def _sigmoid(a):
    return 1.0 / (1.0 + jnp.exp(-a))


def _pack_rows(a):
    half = a.shape[1] // 2

    def bits(v):
        return lax.bitcast_convert_type(v.astype(jnp.bfloat16).astype(jnp.float32), jnp.uint32)

    return bits(a[:, :half]) | (bits(a[:, half:]) >> 16)


def _unpack_rows(p):
    hi = lax.bitcast_convert_type(p & jnp.uint32(0xFFFF0000), jnp.float32)
    lo = lax.bitcast_convert_type(p << 16, jnp.float32)
    return hi, lo


def _moe_residual(x_ref, g_refs, gate, out_lo_hi):
    half = x_ref.shape[1] // 2
    acc_hi, acc_lo = None, None
    for g_ref in g_refs:
        y_hi, y_lo = _unpack_rows(g_ref[...])
        acc_hi = y_hi if acc_hi is None else acc_hi + y_hi
        acc_lo = y_lo if acc_lo is None else acc_lo + y_lo
    out_lo_hi(x_ref[:, 0:half] + gate[:, 0:half] * acc_hi,
              x_ref[:, half:] + gate[:, half:] * acc_lo)


def _mod_kernel(c_ref, w_ref, b_ref, o_ref):
    c = c_ref[...]
    s = c * _sigmoid(c)
    s_hi, s_lo = _split_bf16(s)
    w_hi, w_lo = _split_bf16(w_ref[...])
    dot = functools.partial(jnp.dot, preferred_element_type=jnp.float32)
    o_ref[...] = dot(s_hi, w_hi) + dot(s_lo, w_hi) + dot(s_hi, w_lo) + b_ref[...]


def _modulation(c, w_ada, b_ada):
    depth, d, d6 = w_ada.shape
    bsz = c.shape[0]
    rows = -(-bsz // V7X_SUBLANES) * V7X_SUBLANES
    c_pad = jnp.zeros((rows, d), jnp.float32).at[:bsz].set(c)
    n_col = d6 // d
    out = pl.pallas_call(
        _mod_kernel,
        grid=(depth, n_col),
        in_specs=[
            pl.BlockSpec((rows, d), lambda l, j: (0, 0)),
            pl.BlockSpec((None, d, d), lambda l, j: (l, 0, j)),
            pl.BlockSpec((None, 1, d), lambda l, j: (l, 0, j)),
        ],
        out_specs=pl.BlockSpec((None, rows, d), lambda l, j: (l, 0, j)),
        out_shape=jax.ShapeDtypeStruct((depth, rows, d6), jnp.float32),
        name="adaln_modulation",
    )(c_pad, w_ada, b_ada.reshape(depth, 1, d6))
    return out[:, :bsz].reshape(depth, bsz, 1, d6)


def _mixer_kernel(*refs, capacity, fused):
    n_in = 12 + (TOP_K + 1 if fused else 0)
    (x_ref, mod_ref, gmix_ref, win_ref, conv_ref, wsp_ref, bsp_ref, lng_ref, wout_ref,
     gffn_ref, rwt_ref, rb_ref) = refs[:12]
    if fused:
        g_refs = refs[12:12 + TOP_K]
        pmod_ref = refs[12 + TOP_K]
    x1_ref, cnt_ref, xs_ref = refs[n_in:n_in + 3]
    (hbuf, zbuf, halo_x, halo_c, mabuf, mbuf, run_ref, h2buf, dest_v, dest_s, tag_v,
     dest_sem, row_sem) = refs[n_in + 3:n_in + 16]
    if fused:
        xbuf = refs[n_in + 16]
    tm, d = x_ref.shape
    half = d // 2
    n_exp = rwt_ref.shape[0]
    n_chunks = tm // ROW_CHUNK
    step = pl.program_id(0) * pl.num_programs(1) + pl.program_id(1)
    n_steps = pl.num_programs(0) * pl.num_programs(1)
    slot = step % 2
    dot = functools.partial(jnp.dot, preferred_element_type=jnp.float32)

    @pl.when(pl.program_id(1) == 0)
    def _():
        halo_x[...] = jnp.zeros_like(halo_x)
        halo_c[...] = jnp.zeros_like(halo_c)

    def row_copy(which, t, k, row):
        return pltpu.make_async_copy(h2buf.at[which, pl.ds(t, 1)], xs_ref.at[pl.ds(row, 1)],
                                     row_sem.at[which])

    def tile_rows_wait(which):
        for _ in range(TOP_K):
            pltpu.make_async_copy(h2buf.at[which], xs_ref.at[pl.ds(0, tm)],
                                  row_sem.at[which]).wait()

    def dest_to_smem():
        dest_copy = pltpu.make_async_copy(dest_v, dest_s, dest_sem)
        dest_copy.start()
        dest_copy.wait()

    @pl.when(step == 0)
    def _():
        run_ref[...] = jnp.zeros_like(run_ref)
        tag_v[...] = jnp.zeros_like(tag_v)
        h2buf[1] = jnp.zeros((tm, half + TAG), jnp.uint32)
        spare = (n_exp * capacity + lax.broadcasted_iota(jnp.int32, (TOP_K, tm), 0) * tm
                 + lax.broadcasted_iota(jnp.int32, (TOP_K, tm), 1))
        dest_v[...] = spare
        dest_to_smem()

    def issue_rows(lo, hi):
        for t in range(lo, hi):
            for k in range(TOP_K):
                row_copy(1 - slot, t, k, dest_s[k, t]).start()

    phase_rows = (0, (tm * 7) // 16, (tm * 12) // 16, tm)

    mod = mod_ref[...]
    sh_m, sc_m, g_m = mod[:, 0:d], mod[:, d:2 * d], mod[:, 2 * d:3 * d]
    sh_f, sc_f = mod[:, 3 * d:4 * d], mod[:, 4 * d:5 * d]

    if fused:
        def to_xbuf(x_hi, x_lo):
            xbuf[:, 0:half] = x_hi
            xbuf[:, half:d] = x_lo

        _moe_residual(x_ref, g_refs, pmod_ref[:, 5 * d:6 * d], to_xbuf)
        xin = xbuf
    else:
        xin = x_ref

    x = xin[...]
    gs = gmix_ref[...] * (1.0 + sc_m)
    r = lax.rsqrt(jnp.mean(x * x, axis=-1, keepdims=True) + NORM_EPS)
    hbuf[...] = (x * r * gs + sh_m).astype(jnp.bfloat16)

    h = hbuf[...]
    issue_rows(phase_rows[0], phase_rows[1])
    zbuf[:, 0:3 * d] = dot(h, win_ref[:, 0:3 * d])
    zbuf[:, 3 * d:4 * d] = dot(h, win_ref[:, 5 * d:6 * d])

    cw = conv_ref[...]
    w0, w1, w2 = cw[0:1, :], cw[1:2, :], cw[2:3, :]
    row8 = lax.broadcasted_iota(jnp.int32, (V7X_SUBLANES, d), 0)

    def branch_a(i, carry):
        r0 = pl.multiple_of(i * ROW_CHUNK, ROW_CHUNK)
        rows = pl.ds(r0, ROW_CHUNK)
        p0 = zbuf[rows, 0:d] * zbuf[rows, 2 * d:3 * d]
        rp = pl.ds(pl.multiple_of(jnp.maximum(r0 - V7X_SUBLANES, 0), V7X_SUBLANES), V7X_SUBLANES)
        prev_in = zbuf[rp, 0:d] * zbuf[rp, 2 * d:3 * d]
        prev = jnp.where(i == 0, halo_x[...] * halo_c[...], prev_in)
        r1 = pltpu.roll(p0, 1, 0)
        r2 = pltpu.roll(p0, 2, 0)
        q1 = pltpu.roll(prev, 1, 0)
        q2 = pltpu.roll(prev, 2, 0)
        p1 = jnp.concatenate([jnp.where(row8 < 1, q1, r1[0:V7X_SUBLANES]), r1[V7X_SUBLANES:]], axis=0)
        p2 = jnp.concatenate([jnp.where(row8 < 2, q2, r2[0:V7X_SUBLANES]), r2[V7X_SUBLANES:]], axis=0)
        y_a = zbuf[rows, d:2 * d] * (w0 * p2 + w1 * p1 + w2 * p0)
        mabuf[rows, :] = _sigmoid(zbuf[rows, 3 * d:4 * d]) * y_a
        return carry

    lax.fori_loop(0, n_chunks, branch_a, 0)
    last8 = pl.ds(tm - V7X_SUBLANES, V7X_SUBLANES)
    halo_x[...] = zbuf[last8, 0:d]
    halo_c[...] = zbuf[last8, 2 * d:3 * d]

    issue_rows(phase_rows[1], phase_rows[2])
    zbuf[:, 0:2 * d] = dot(h, win_ref[:, 3 * d:5 * d])
    zbuf[:, 2 * d:3 * d] = dot(h, win_ref[:, 6 * d:7 * d])

    tri = (lax.broadcasted_iota(jnp.int32, (CHUNK, CHUNK), 1)
           <= lax.broadcasted_iota(jnp.int32, (CHUNK, CHUNK), 0))
    wm = [jnp.where(tri, wsp_ref[g], 0.0).astype(jnp.bfloat16) for g in range(GROUPS)]
    gd = d // GROUPS
    lng = lng_ref[...]

    def branch_b(i, carry):
        r0 = pl.multiple_of(i * ROW_CHUNK, ROW_CHUNK)
        rows = pl.ds(r0, ROW_CHUNK)
        gv = _gelu(zbuf[rows, d:2 * d])
        mu = jnp.mean(gv, axis=-1, keepdims=True)
        vc = gv - mu
        var = jnp.mean(vc * vc, axis=-1, keepdims=True)
        vn = (vc * lax.rsqrt(var + NORM_EPS) * lng).astype(jnp.bfloat16)
        s = jnp.concatenate(
            [dot(wm[g], vn[:, g * gd:(g + 1) * gd]) for g in range(GROUPS)], axis=1)
        y_b = _gelu(zbuf[rows, 0:d]) * (s + bsp_ref[...])
        m = mabuf[rows, :] + _sigmoid(zbuf[rows, 2 * d:3 * d]) * y_b
        mbuf[rows, :] = m.astype(jnp.bfloat16)
        return carry

    lax.fori_loop(0, n_chunks, branch_b, 0)

    @pl.when(step > 0)
    def _():
        tile_rows_wait(slot)

    issue_rows(phase_rows[2], phase_rows[3])
    x1 = xin[...] + g_m * dot(mbuf[...], wout_ref[...])
    x1_ref[...] = x1
    gs2 = gffn_ref[...] * (1.0 + sc_f)
    r2n = lax.rsqrt(jnp.mean(x1 * x1, axis=-1, keepdims=True) + NORM_EPS)
    h2 = x1 * r2n * gs2 + sh_f

    h2buf[slot, :, 0:half] = _pack_rows(h2)

    nt = (((1,), (1,)), ((), ()))
    dg = functools.partial(lax.dot_general, dimension_numbers=nt,
                           preferred_element_type=jnp.float32)
    h_hi, h_lo = _split_bf16(h2)
    r_hi, r_lo = _split_bf16(rwt_ref[...])
    logits = dg(r_hi, h_hi) + dg(r_hi, h_lo) + dg(r_lo, h_hi) + rb_ref[...]

    iota_e = lax.broadcasted_iota(jnp.int32, (n_exp, tm), 0)
    vals = logits
    tops, idxs, hots = [], [], []
    for _ in range(TOP_K):
        mx = jnp.max(vals, axis=0, keepdims=True)
        ix = jnp.min(jnp.where(vals == mx, iota_e, n_exp), axis=0, keepdims=True)
        hot = iota_e == ix
        vals = jnp.where(hot, -jnp.inf, vals)
        tops.append(mx)
        idxs.append(ix)
        hots.append(hot)
    exps = [jnp.exp(t - tops[0]) for t in tops]
    denom = exps[0] + exps[1] + exps[2] + exps[3]

    member = jnp.where(hots[0] | hots[1] | hots[2] | hots[3], 1.0, 0.0)
    upper = (lax.broadcasted_iota(jnp.int32, (tm, tm), 0)
             < lax.broadcasted_iota(jnp.int32, (tm, tm), 1))
    before = dot(member.astype(jnp.bfloat16), jnp.where(upper, 1.0, 0.0).astype(jnp.bfloat16))
    run = run_ref[...]
    pos = before + run[:, 0:1]
    token = step * tm + lax.broadcasted_iota(jnp.int32, (1, tm), 1)
    for k in range(TOP_K):
        rk = jnp.sum(jnp.where(hots[k], pos, 0.0), axis=0, keepdims=True)
        dest_v[k:k + 1, :] = idxs[k] * capacity + rk.astype(jnp.int32)
        tag_v[k:k + 1, :] = idxs[k]
        tag_v[TOP_K + k:TOP_K + k + 1, :] = k * capacity + token
        tag_v[2 * TOP_K + k:2 * TOP_K + k + 1, :] = lax.bitcast_convert_type(
            exps[k] / denom, jnp.int32)
    run = run + jnp.sum(member, axis=1, keepdims=True)
    run_ref[...] = run
    cnt_ref[...] = run.astype(jnp.int32)
    h2buf[slot, :, half:half + TAG] = lax.bitcast_convert_type(tag_v[...].T, jnp.uint32)

    dest_to_smem()

    @pl.when(step == n_steps - 1)
    def _():
        def issue(t, carry):
            for k in range(TOP_K):
                row_copy(slot, t, k, dest_s[k, t]).start()
            return carry

        lax.fori_loop(0, tm, issue, 0)
        tile_rows_wait(1 - slot)
        tile_rows_wait(slot)


def _result_specs(n_tok, tm, tiles, half):
    def spec(k):
        return pl.BlockSpec((tm, half), lambda b, s: (k * (n_tok // tm) + b * tiles + s, 0))
    return [spec(k) for k in range(TOP_K)]


def _mixer(x, mod_l, gmix, win_bf, conv_w, wsp, bsp_full, lng, wout_bf, gffn, rwt, rb, prev):
    bsz, seq, d = x.shape
    n_exp = rwt.shape[0]
    tm = MIX_TILE
    n_tok = bsz * seq
    tiles = seq // tm
    half = d // 2
    fused = prev is not None
    const = pl.Buffered(1)

    def cspec(shape):
        return pl.BlockSpec(shape, lambda b, s: (0,) * len(shape), pipeline_mode=const)

    out_shape = (
        jax.ShapeDtypeStruct((bsz, seq, d), jnp.float32),
        jax.ShapeDtypeStruct((n_exp, V7X_LANES), jnp.int32),
        jax.ShapeDtypeStruct((n_exp * n_tok + TOP_K * tm, half + TAG), jnp.uint32),
    )
    in_specs = [
        pl.BlockSpec((None, tm, d), lambda b, s: (b, s, 0)),
        pl.BlockSpec((None, 1, 6 * d), lambda b, s: (b, 0, 0)),
        cspec((1, d)),
        cspec(win_bf.shape),
        cspec(conv_w.shape),
        cspec(wsp.shape),
        cspec(bsp_full.shape),
        cspec((1, d)),
        cspec(wout_bf.shape),
        cspec((1, d)),
        cspec(rwt.shape),
        cspec(rb.shape),
    ]
    args = [x, mod_l, gmix, win_bf, conv_w, wsp, bsp_full, lng, wout_bf, gffn, rwt, rb]
    scratch = [
        pltpu.VMEM((tm, d), jnp.bfloat16),
        pltpu.VMEM((tm, 4 * d), jnp.float32),
        pltpu.VMEM((V7X_SUBLANES, d), jnp.float32),
        pltpu.VMEM((V7X_SUBLANES, d), jnp.float32),
        pltpu.VMEM((tm, d), jnp.float32),
        pltpu.VMEM((tm, d), jnp.bfloat16),
        pltpu.VMEM((n_exp, V7X_LANES), jnp.float32),
        pltpu.VMEM((2, tm, half + TAG), jnp.uint32),
        pltpu.VMEM((TOP_K, tm), jnp.int32),
        pltpu.SMEM((TOP_K, tm), jnp.int32),
        pltpu.VMEM((TAG, tm), jnp.int32),
        pltpu.SemaphoreType.DMA(()),
        pltpu.SemaphoreType.DMA((2,)),
    ]
    if fused:
        g, pmod = prev
        in_specs += _result_specs(n_tok, tm, tiles, half)
        in_specs += [pl.BlockSpec((None, 1, 6 * d), lambda b, s: (b, 0, 0))]
        args += [g] * TOP_K + [pmod]
        scratch += [pltpu.VMEM((tm, d), jnp.float32)]
    return pl.pallas_call(
        functools.partial(_mixer_kernel, capacity=n_tok, fused=fused),
        grid=(bsz, tiles),
        in_specs=in_specs,
        out_specs=(
            pl.BlockSpec((None, tm, d), lambda b, s: (b, s, 0)),
            pl.BlockSpec((n_exp, V7X_LANES), lambda b, s: (0, 0)),
            pl.BlockSpec(memory_space=pl.ANY),
        ),
        out_shape=out_shape,
        scratch_shapes=scratch,
        compiler_params=pltpu.CompilerParams(
            dimension_semantics=("arbitrary", "arbitrary"),
            vmem_limit_bytes=V7X_VMEM_LIMIT_BYTES),
        name="residual_mixer_router_dispatch" if fused else "mixer_router_dispatch",
    )(*args)


def _experts_kernel(blk_ref, exp_ref, nv_ref,
                    xs_ref, wgu_ref, bgu_ref, wdn_ref, bdn_ref, g_ref,
                    wgu_bf, wdn_bf, ycur, ysend, gid_v, gid_s, gid_sem, row_sem, *, n_result):
    i = pl.program_id(0)
    n_items = pl.num_programs(0)
    rows = xs_ref.shape[0]
    half = xs_ref.shape[1] - TAG
    d_ff = wdn_ref.shape[0]
    prev = jnp.maximum(i - 1, 0)
    expert = exp_ref[i]
    new_expert = jnp.logical_or(i == 0, expert != exp_ref[prev])
    n_valid = nv_ref[i]
    active = n_valid > 0
    prev_active = jnp.logical_and(i > 0, nv_ref[prev] > 0)

    def row_copy(t, row):
        return pltpu.make_async_copy(ysend.at[pl.ds(t, 1)], g_ref.at[pl.ds(row, 1)], row_sem)

    def block_rows_wait():
        pltpu.make_async_copy(ysend, g_ref.at[pl.ds(0, rows)], row_sem).wait()

    def gid_to_smem():
        gid_copy = pltpu.make_async_copy(gid_v, gid_s, gid_sem)
        gid_copy.start()
        gid_copy.wait()

    def send_last_block():
        block_rows_wait()
        ysend[...] = ycur[...]

        def issue(t, carry):
            row_copy(t, gid_s[0, t]).start()
            return carry

        lax.fori_loop(0, rows, issue, 0)
        block_rows_wait()

    @pl.when(i == 0)
    def _():
        ysend[...] = jnp.zeros((rows, half), jnp.uint32)
        gid_v[...] = n_result + lax.broadcasted_iota(jnp.int32, (1, rows), 1)
        gid_to_smem()

    @pl.when(new_expert)
    def _():
        wgu_bf[...] = wgu_ref[...].astype(jnp.bfloat16)
        wdn_bf[...] = wdn_ref[...].astype(jnp.bfloat16)

    @pl.when(jnp.logical_and(active, i > 0))
    def _():
        block_rows_wait()
        ysend[...] = ycur[...]

    @pl.when(active)
    def _():
        def issue_rows(lo, hi):
            for t in range(lo, hi):
                row_copy(t, gid_s[0, t]).start()

        ridx = lax.broadcasted_iota(jnp.int32, (rows, half), 0)
        packed = jnp.where(ridx < n_valid, xs_ref[:, 0:half], jnp.uint32(0))
        x_hi, x_lo = _unpack_rows(packed)
        xs = jnp.concatenate([x_hi.astype(jnp.bfloat16), x_lo.astype(jnp.bfloat16)], axis=1)
        n_gu = d_ff // MXU_COLS
        n_dn = 2 * half // MXU_COLS
        per = rows // (n_gu + n_dn)
        acts = []
        for j in range(n_gu):
            issue_rows(j * per, (j + 1) * per)
            c0, c1 = j * MXU_COLS, (j + 1) * MXU_COLS
            gate = jnp.dot(xs, wgu_bf[:, c0:c1], preferred_element_type=jnp.float32)
            up = jnp.dot(xs, wgu_bf[:, d_ff + c0:d_ff + c1], preferred_element_type=jnp.float32)
            gate = jnp.minimum(gate + bgu_ref[:, c0:c1], SWIGLU_LIMIT)
            up = jnp.clip(up + bgu_ref[:, d_ff + c0:d_ff + c1], -SWIGLU_LIMIT, SWIGLU_LIMIT)
            acts.append((gate * _sigmoid(SWIGLU_ALPHA * gate) * (up + 1.0)).astype(jnp.bfloat16))
        act = jnp.concatenate(acts, axis=1)
        ys = []
        for j in range(n_dn):
            issue_rows((n_gu + j) * per, (n_gu + j + 1) * per)
            c0, c1 = j * MXU_COLS, (j + 1) * MXU_COLS
            ys.append(jnp.dot(act, wdn_bf[:, c0:c1], preferred_element_type=jnp.float32)
                      + bdn_ref[:, c0:c1])
        y = jnp.concatenate(ys, axis=1)

        tag = lax.bitcast_convert_type(xs_ref[:, half:half + TAG], jnp.int32)
        lane = lax.broadcasted_iota(jnp.int32, (rows, TAG), 1)
        mine = jnp.logical_and(tag == expert, lane < TOP_K)
        dst = pltpu.roll(tag, TAG - TOP_K, 1)
        wgt = lax.bitcast_convert_type(pltpu.roll(tag, TAG - 2 * TOP_K, 1), jnp.float32)
        dst_col = jnp.sum(jnp.where(mine, dst, 0), axis=1, keepdims=True)
        wgt_col = jnp.sum(jnp.where(mine, wgt, 0.0), axis=1, keepdims=True)
        row_id = lax.broadcasted_iota(jnp.int32, (rows, TAG), 0)
        dst_full = jnp.where(row_id < n_valid, dst_col, n_result + row_id)

        ycur[...] = _pack_rows(y * wgt_col)
        gid_v[...] = dst_full.T[0:1, :]
        gid_to_smem()

    @pl.when(jnp.logical_or(jnp.logical_and(jnp.logical_not(active), prev_active),
                            jnp.logical_and(active, i == n_items - 1)))
    def _():
        send_last_block()


def _experts(items, xs, w_gu, b_gu, w_dn, b_dn, layer, n_tok):
    n_rows, width = xs.shape
    half = width - TAG
    d = 2 * half
    blk, exp, n_valid = items
    n_items = blk.shape[0]
    rows = EXPERT_BLOCK
    d_ff2 = w_gu.shape[-1]
    d_ff = w_dn.shape[-2]
    n_exp = w_gu.shape[1]
    depth = w_gu.shape[0]
    n_result = TOP_K * n_tok
    wmap = lambda i, blk, exp, nv: (layer, exp[i], 0, 0)
    grid_spec = pltpu.PrefetchScalarGridSpec(
        num_scalar_prefetch=3,
        grid=(n_items,),
        in_specs=[
            pl.BlockSpec((rows, width), lambda i, blk, exp, nv: (blk[i], 0)),
            pl.BlockSpec((None, None, d, d_ff2), wmap),
            pl.BlockSpec((None, None, 1, d_ff2), wmap),
            pl.BlockSpec((None, None, d_ff, d), wmap),
            pl.BlockSpec((None, None, 1, d), wmap),
        ],
        out_specs=pl.BlockSpec(memory_space=pl.ANY),
        scratch_shapes=[
            pltpu.VMEM((d, d_ff2), jnp.bfloat16),
            pltpu.VMEM((d_ff, d), jnp.bfloat16),
            pltpu.VMEM((rows, half), jnp.uint32),
            pltpu.VMEM((rows, half), jnp.uint32),
            pltpu.VMEM((1, rows), jnp.int32),
            pltpu.SMEM((1, rows), jnp.int32),
            pltpu.SemaphoreType.DMA(()),
            pltpu.SemaphoreType.DMA(()),
        ],
    )
    return pl.pallas_call(
        functools.partial(_experts_kernel, n_result=n_result),
        grid_spec=grid_spec,
        out_shape=jax.ShapeDtypeStruct((n_result + rows, half), jnp.uint32),
        compiler_params=pltpu.CompilerParams(
            dimension_semantics=("arbitrary",),
            vmem_limit_bytes=V7X_VMEM_LIMIT_BYTES),
        name="experts",
    )(blk, exp, n_valid, xs, w_gu, b_gu.reshape(depth, n_exp, 1, d_ff2), w_dn,
      b_dn.reshape(depth, n_exp, 1, d))


def _expert_items(counts, capacity, n_assign):
    n_exp = counts.shape[0]
    rows = EXPERT_BLOCK
    n_items = n_assign // rows + n_exp
    per_exp = (counts + rows - 1) // rows
    item_end = jnp.cumsum(per_exp)
    item_start = item_end - per_exp
    total = item_end[-1]
    idx = jnp.arange(n_items, dtype=jnp.int32)
    pick = jnp.minimum(idx, total - 1)
    exp = jnp.sum((item_end[None, :] <= pick[:, None]).astype(jnp.int32), axis=1)
    sub = pick - item_start[exp]
    blk = exp * (capacity // rows) + sub
    n_valid = jnp.where(idx < total, jnp.minimum(counts[exp] - sub * rows, rows), 0)
    return blk.astype(jnp.int32), exp.astype(jnp.int32), n_valid.astype(jnp.int32)


def _final_kernel(*refs):
    x1_ref = refs[0]
    g_refs = refs[1:1 + TOP_K]
    mod_ref, fin_ref, o_ref = refs[1 + TOP_K:]
    d = x1_ref.shape[1]
    half = d // 2

    def finish(x_hi, x_lo):
        ms = (jnp.sum(x_hi * x_hi, axis=-1, keepdims=True)
              + jnp.sum(x_lo * x_lo, axis=-1, keepdims=True)) * (1.0 / d)
        r = lax.rsqrt(ms + NORM_EPS)
        o_ref[:, 0:half] = x_hi * r * fin_ref[:, 0:half]
        o_ref[:, half:d] = x_lo * r * fin_ref[:, half:d]

    _moe_residual(x1_ref, g_refs, mod_ref[:, 5 * d:6 * d], finish)


def _final(x1, g, mod_l, fin_g):
    bsz, seq, d = x1.shape
    tm = MIX_TILE
    tiles = seq // tm
    n_tok = bsz * seq
    return pl.pallas_call(
        _final_kernel,
        grid=(bsz, tiles),
        in_specs=([pl.BlockSpec((None, tm, d), lambda b, s: (b, s, 0))]
                  + _result_specs(n_tok, tm, tiles, d // 2)
                  + [pl.BlockSpec((None, 1, 6 * d), lambda b, s: (b, 0, 0)),
                     pl.BlockSpec((1, d), lambda b, s: (0, 0))]),
        out_specs=pl.BlockSpec((None, tm, d), lambda b, s: (b, s, 0)),
        out_shape=jax.ShapeDtypeStruct((bsz, seq, d), jnp.float32),
        compiler_params=pltpu.CompilerParams(dimension_semantics=("arbitrary", "arbitrary")),
        name="final_residual_norm",
    )(x1, *([g] * TOP_K), mod_l, fin_g)


def kernel(x, c, w_ada, b_ada, norm_mix_g, w_in, conv_w, w_spatial, b_spatial, ln_v_g, w_out,
           norm_ffn_g, router_w, router_b, w_gate_up, b_gate_up, w_down, b_down, final_norm_g):
    bsz, seq, d = x.shape
    depth = w_ada.shape[0]
    n_exp = router_w.shape[-1]
    n_tok = bsz * seq
    assert seq % MIX_TILE == 0 and MIX_TILE % CHUNK == 0 and n_tok % EXPERT_BLOCK == 0

    mod = _modulation(c, w_ada, b_ada)
    win_bf = w_in.astype(jnp.bfloat16)
    wout_bf = w_out.astype(jnp.bfloat16)
    bsp_full = jnp.repeat(jnp.swapaxes(b_spatial, 1, 2), d // GROUPS, axis=2)
    rwt = jnp.swapaxes(router_w, 1, 2)

    prev = None
    for l in range(depth):
        x, cnt, xs = _mixer(
            x, mod[l], norm_mix_g[l].reshape(1, d), win_bf[l], conv_w[l], w_spatial[l],
            bsp_full[l], ln_v_g[l].reshape(1, d), wout_bf[l], norm_ffn_g[l].reshape(1, d),
            rwt[l], router_b[l].reshape(n_exp, 1), prev)
        items = _expert_items(cnt[:, 0], n_tok, n_tok * TOP_K)
        g = _experts(items, xs, w_gate_up, b_gate_up, w_down, b_down, l, n_tok)
        prev = (g, mod[l])
    return _final(x, g, mod[depth - 1], final_norm_g.reshape(1, d))
```

```python
import functools
import math

import jax
import jax.numpy as jnp
from jax import lax
from jax.experimental import pallas as pl
from jax.experimental.pallas import tpu as pltpu

CONV_K = 3
GROUPS = 8
CHUNK = 128
TOP_K = 4
SWIGLU_LIMIT = 7.0
SWIGLU_ALPHA = 1.702
NORM_EPS = 1e-6

V7X_SUBLANES = 8
V7X_LANES = 128
V7X_VMEM_LIMIT_BYTES = 56 * 1024 * 1024

MIX_TILE = 512
ROW_CHUNK = CHUNK
COMBINE_TILE = 256
EXPERT_BLOCK = 512
PIECES = 4


def _split_bf16(a):
    hi = a.astype(jnp.bfloat16)
    lo = (a - hi.astype(jnp.float32)).astype(jnp.bfloat16)
    return hi, lo


def _gelu(a):
    return 0.5 * a * (1.0 + lax.erf(a * (1.0 / math.sqrt(2.0))))


def _sigmoid(a):
    return 1.0 / (1.0 + jnp.exp(-a))


def _pack_rows(a):
    half = a.shape[1] // 2

    def bits(v):
        return lax.bitcast_convert_type(v.astype(jnp.bfloat16).astype(jnp.float32), jnp.uint32)

    return bits(a[:, :half]) | (bits(a[:, half:]) >> 16)


def _unpack_rows(p):
    hi = lax.bitcast_convert_type(p & jnp.uint32(0xFFFF0000), jnp.float32)
    lo = lax.bitcast_convert_type(p << 16, jnp.float32)
    return hi, lo


def _store_rows(ref, packed):
    rows = packed.shape[0]
    for c in range(PIECES):
        ref[pl.ds(c, rows, stride=PIECES), :] = packed[:, c * V7X_LANES:(c + 1) * V7X_LANES]


def _load_rows(ref, rows):
    return jnp.concatenate([ref[pl.ds(c, rows, stride=PIECES), :] for c in range(PIECES)], axis=1)


def _mod_kernel(c_ref, w_ref, b_ref, o_ref):
    c = c_ref[...]
    s = c * _sigmoid(c)
    s_hi, s_lo = _split_bf16(s)
    w_hi, w_lo = _split_bf16(w_ref[...])
    dot = functools.partial(jnp.dot, preferred_element_type=jnp.float32)
    o_ref[...] = dot(s_hi, w_hi) + dot(s_lo, w_hi) + dot(s_hi, w_lo) + b_ref[...]


def _modulation(c, w_ada, b_ada):
    depth, d, d6 = w_ada.shape
    bsz = c.shape[0]
    rows = -(-bsz // V7X_SUBLANES) * V7X_SUBLANES
    c_pad = jnp.zeros((rows, d), jnp.float32).at[:bsz].set(c)
    n_col = d6 // d
    out = pl.pallas_call(
        _mod_kernel,
        grid=(depth, n_col),
        in_specs=[
            pl.BlockSpec((rows, d), lambda l, j: (0, 0)),
            pl.BlockSpec((None, d, d), lambda l, j: (l, 0, j)),
            pl.BlockSpec((None, 1, d), lambda l, j: (l, 0, j)),
        ],
        out_specs=pl.BlockSpec((None, rows, d), lambda l, j: (l, 0, j)),
        out_shape=jax.ShapeDtypeStruct((depth, rows, d6), jnp.float32),
        name="adaln_modulation",
    )(c_pad, w_ada, b_ada.reshape(depth, 1, d6))
    return out[:, :bsz].reshape(depth, bsz, 1, d6)


def _mixer_kernel(*refs, capacity, fused):
    if fused:
        (x_ref, mod_ref, gmix_ref, win_ref, conv_ref, wsp_ref, bsp_ref, lng_ref, wout_ref,
         gffn_ref, rwt_ref, rb_ref, pdest_cur, pdest_nxt, pw_ref, pmod_ref, y_ref,
         x1_ref, dest_ref, tw_ref, cnt_ref, xs_ref,
         hbuf, zbuf, halo_x, halo_c, mabuf, mbuf, run_ref, h2buf, dest_v, dest_s,
         dest_sem, row_sem, xbuf, gbuf, gat_sem) = refs
    else:
        (x_ref, mod_ref, gmix_ref, win_ref, conv_ref, wsp_ref, bsp_ref, lng_ref, wout_ref,
         gffn_ref, rwt_ref, rb_ref,
         x1_ref, dest_ref, tw_ref, cnt_ref, xs_ref,
         hbuf, zbuf, halo_x, halo_c, mabuf, mbuf, run_ref, h2buf, dest_v, dest_s,
         dest_sem, row_sem) = refs
    tm, d = x_ref.shape
    half = d // 2
    n_exp = rwt_ref.shape[0]
    n_chunks = tm // ROW_CHUNK
    step = pl.program_id(0) * pl.num_programs(1) + pl.program_id(1)
    n_steps = pl.num_programs(0) * pl.num_programs(1)
    slot = step % 2
    dot = functools.partial(jnp.dot, preferred_element_type=jnp.float32)

    @pl.when(pl.program_id(1) == 0)
    def _():
        halo_x[...] = jnp.zeros_like(halo_x)
        halo_c[...] = jnp.zeros_like(halo_c)

    def row_copy(which, t, k, row):
        return pltpu.make_async_copy(h2buf.at[which, pl.ds(t * PIECES, PIECES)],
                                     xs_ref.at[pl.ds(row, PIECES)], row_sem.at[which])

    def tile_rows_wait(which):
        for _ in range(TOP_K):
            pltpu.make_async_copy(h2buf.at[which], xs_ref.at[pl.ds(0, tm * PIECES)],
                                  row_sem.at[which]).wait()

    def dest_to_smem():
        dest_copy = pltpu.make_async_copy(dest_v, dest_s, dest_sem)
        dest_copy.start()
        dest_copy.wait()

    def gather_copy(which, t, k, row):
        return pltpu.make_async_copy(y_ref.at[pl.ds(row, PIECES)],
                                     gbuf.at[which, k, pl.ds(t * PIECES, PIECES)],
                                     gat_sem.at[which])

    def tile_gather_wait(which):
        for k in range(TOP_K):
            pltpu.make_async_copy(y_ref.at[pl.ds(0, tm * PIECES)], gbuf.at[which, k],
                                  gat_sem.at[which]).wait()

    @pl.when(step == 0)
    def _():
        run_ref[...] = jnp.zeros_like(run_ref)
        h2buf[1] = jnp.zeros((tm * PIECES, V7X_LANES), jnp.uint32)
        spare = (n_exp * capacity + lax.broadcasted_iota(jnp.int32, (TOP_K, tm), 0) * tm
                 + lax.broadcasted_iota(jnp.int32, (TOP_K, tm), 1))
        dest_v[...] = spare * PIECES
        dest_to_smem()
        if fused:
            def first_gather(t, carry):
                for k in range(TOP_K):
                    gather_copy(0, t, k, pdest_cur[k * tm + t]).start()
                return carry

            lax.fori_loop(0, tm, first_gather, 0)

    def issue_rows(lo, hi):
        for t in range(lo, hi):
            for k in range(TOP_K):
                row_copy(1 - slot, t, k, dest_s[k, t]).start()
                if fused:
                    gather_copy(1 - slot, t, k, pdest_nxt[k * tm + t]).start()

    phase_rows = (0, (tm * 7) // 16, (tm * 12) // 16, tm)

    mod = mod_ref[...]
    sh_m, sc_m, g_m = mod[:, 0:d], mod[:, d:2 * d], mod[:, 2 * d:3 * d]
    sh_f, sc_f = mod[:, 3 * d:4 * d], mod[:, 4 * d:5 * d]

    if fused:
        tile_gather_wait(slot)
        w = pw_ref[...]
        acc_hi, acc_lo = None, None
        for k in range(TOP_K):
            y_hi, y_lo = _unpack_rows(_load_rows(gbuf.at[slot, k], tm))
            wk = w[:, k:k + 1]
            acc_hi = wk * y_hi if acc_hi is None else acc_hi + wk * y_hi
            acc_lo = wk * y_lo if acc_lo is None else acc_lo + wk * y_lo
        g_prev = pmod_ref[:, 5 * d:6 * d]
        xbuf[:, 0:half] = x_ref[:, 0:half] + g_prev[:, 0:half] * acc_hi
        xbuf[:, half:d] = x_ref[:, half:d] + g_prev[:, half:d] * acc_lo
        xin = xbuf
    else:
        xin = x_ref

    x = xin[...]
    gs = gmix_ref[...] * (1.0 + sc_m)
    r = lax.rsqrt(jnp.mean(x * x, axis=-1, keepdims=True) + NORM_EPS)
    hbuf[...] = (x * r * gs + sh_m).astype(jnp.bfloat16)

    h = hbuf[...]
    issue_rows(phase_rows[0], phase_rows[1])
    zbuf[:, 0:3 * d] = dot(h, win_ref[:, 0:3 * d])
    zbuf[:, 3 * d:4 * d] = dot(h, win_ref[:, 5 * d:6 * d])

    cw = conv_ref[...]
    w0, w1, w2 = cw[0:1, :], cw[1:2, :], cw[2:3, :]
    row8 = lax.broadcasted_iota(jnp.int32, (V7X_SUBLANES, d), 0)

    def branch_a(i, carry):
        r0 = pl.multiple_of(i * ROW_CHUNK, ROW_CHUNK)
        rows = pl.ds(r0, ROW_CHUNK)
        p0 = zbuf[rows, 0:d] * zbuf[rows, 2 * d:3 * d]
        rp = pl.ds(pl.multiple_of(jnp.maximum(r0 - V7X_SUBLANES, 0), V7X_SUBLANES), V7X_SUBLANES)
        prev_in = zbuf[rp, 0:d] * zbuf[rp, 2 * d:3 * d]
        prev = jnp.where(i == 0, halo_x[...] * halo_c[...], prev_in)
        r1 = pltpu.roll(p0, 1, 0)
        r2 = pltpu.roll(p0, 2, 0)
        q1 = pltpu.roll(prev, 1, 0)
        q2 = pltpu.roll(prev, 2, 0)
        p1 = jnp.concatenate([jnp.where(row8 < 1, q1, r1[0:V7X_SUBLANES]), r1[V7X_SUBLANES:]], axis=0)
        p2 = jnp.concatenate([jnp.where(row8 < 2, q2, r2[0:V7X_SUBLANES]), r2[V7X_SUBLANES:]], axis=0)
        y_a = zbuf[rows, d:2 * d] * (w0 * p2 + w1 * p1 + w2 * p0)
        mabuf[rows, :] = _sigmoid(zbuf[rows, 3 * d:4 * d]) * y_a
        return carry

    lax.fori_loop(0, n_chunks, branch_a, 0)
    last8 = pl.ds(tm - V7X_SUBLANES, V7X_SUBLANES)
    halo_x[...] = zbuf[last8, 0:d]
    halo_c[...] = zbuf[last8, 2 * d:3 * d]

    issue_rows(phase_rows[1], phase_rows[2])
    zbuf[:, 0:2 * d] = dot(h, win_ref[:, 3 * d:5 * d])
    zbuf[:, 2 * d:3 * d] = dot(h, win_ref[:, 6 * d:7 * d])

    tri = (lax.broadcasted_iota(jnp.int32, (CHUNK, CHUNK), 1)
           <= lax.broadcasted_iota(jnp.int32, (CHUNK, CHUNK), 0))
    wm = [jnp.where(tri, wsp_ref[g], 0.0).astype(jnp.bfloat16) for g in range(GROUPS)]
    gd = d // GROUPS
    lng = lng_ref[...]

    def branch_b(i, carry):
        r0 = pl.multiple_of(i * ROW_CHUNK, ROW_CHUNK)
        rows = pl.ds(r0, ROW_CHUNK)
        gv = _gelu(zbuf[rows, d:2 * d])
        mu = jnp.mean(gv, axis=-1, keepdims=True)
        vc = gv - mu
        var = jnp.mean(vc * vc, axis=-1, keepdims=True)
        vn = (vc * lax.rsqrt(var + NORM_EPS) * lng).astype(jnp.bfloat16)
        s = jnp.concatenate(
            [dot(wm[g], vn[:, g * gd:(g + 1) * gd]) for g in range(GROUPS)], axis=1)
        y_b = _gelu(zbuf[rows, 0:d]) * (s + bsp_ref[...])
        m = mabuf[rows, :] + _sigmoid(zbuf[rows, 2 * d:3 * d]) * y_b
        mbuf[rows, :] = m.astype(jnp.bfloat16)
        return carry

    lax.fori_loop(0, n_chunks, branch_b, 0)

    @pl.when(step > 0)
    def _():
        tile_rows_wait(slot)

    issue_rows(phase_rows[2], phase_rows[3])
    x1 = xin[...] + g_m * dot(mbuf[...], wout_ref[...])
    x1_ref[...] = x1
    gs2 = gffn_ref[...] * (1.0 + sc_f)
    r2n = lax.rsqrt(jnp.mean(x1 * x1, axis=-1, keepdims=True) + NORM_EPS)
    h2 = x1 * r2n * gs2 + sh_f

    _store_rows(h2buf.at[slot], _pack_rows(h2))

    nt = (((1,), (1,)), ((), ()))
    dg = functools.partial(lax.dot_general, dimension_numbers=nt,
                           preferred_element_type=jnp.float32)
    h_hi, h_lo = _split_bf16(h2)
    r_hi, r_lo = _split_bf16(rwt_ref[...])
    logits = dg(r_hi, h_hi) + dg(r_hi, h_lo) + dg(r_lo, h_hi) + rb_ref[...]

    iota_e = lax.broadcasted_iota(jnp.int32, (n_exp, tm), 0)
    vals = logits
    tops, idxs, hots = [], [], []
    for _ in range(TOP_K):
        mx = jnp.max(vals, axis=0, keepdims=True)
        ix = jnp.min(jnp.where(vals == mx, iota_e, n_exp), axis=0, keepdims=True)
        hot = iota_e == ix
        vals = jnp.where(hot, -jnp.inf, vals)
        tops.append(mx)
        idxs.append(ix)
        hots.append(hot)
    exps = [jnp.exp(t - tops[0]) for t in tops]
    denom = exps[0] + exps[1] + exps[2] + exps[3]
    for k in range(TOP_K):
        tw_ref[k:k + 1, :] = exps[k] / denom

    member = jnp.where(hots[0] | hots[1] | hots[2] | hots[3], 1.0, 0.0)
    upper = (lax.broadcasted_iota(jnp.int32, (tm, tm), 0)
             < lax.broadcasted_iota(jnp.int32, (tm, tm), 1))
    before = dot(member.astype(jnp.bfloat16), jnp.where(upper, 1.0, 0.0).astype(jnp.bfloat16))
    run = run_ref[...]
    pos = before + run[:, 0:1]
    for k in range(TOP_K):
        rk = jnp.sum(jnp.where(hots[k], pos, 0.0), axis=0, keepdims=True)
        dest_k = (idxs[k] * capacity + rk.astype(jnp.int32)) * PIECES
        dest_ref[k:k + 1, :] = dest_k
        dest_v[k:k + 1, :] = dest_k
    run = run + jnp.sum(member, axis=1, keepdims=True)
    run_ref[...] = run
    cnt_ref[...] = run.astype(jnp.int32)

    dest_to_smem()

    @pl.when(step == n_steps - 1)
    def _():
        def issue(t, carry):
            for k in range(TOP_K):
                row_copy(slot, t, k, dest_s[k, t]).start()
            return carry

        lax.fori_loop(0, tm, issue, 0)
        tile_rows_wait(1 - slot)
        tile_rows_wait(slot)
        if fused:
            tile_gather_wait(1 - slot)


def _tile_major(a, tm):
    n_tok = a.shape[1]
    return a.reshape(TOP_K, n_tok // tm, tm).transpose(1, 0, 2).reshape(-1)


def _mixer(x, mod_l, gmix, win_bf, conv_w, wsp, bsp_full, lng, wout_bf, gffn, rwt, rb, prev):
    bsz, seq, d = x.shape
    n_exp = rwt.shape[0]
    tm = MIX_TILE
    n_tok = bsz * seq
    tiles = seq // tm
    n_steps = bsz * tiles
    fused = prev is not None
    const = pl.Buffered(1)

    def cspec(shape):
        return pl.BlockSpec(shape, lambda b, s: (0,) * len(shape), pipeline_mode=const)

    out_shape = (
        jax.ShapeDtypeStruct((bsz, seq, d), jnp.float32),
        jax.ShapeDtypeStruct((TOP_K, n_tok), jnp.int32),
        jax.ShapeDtypeStruct((TOP_K, n_tok), jnp.float32),
        jax.ShapeDtypeStruct((n_exp, V7X_LANES), jnp.int32),
        jax.ShapeDtypeStruct(((n_exp * n_tok + TOP_K * tm) * PIECES, V7X_LANES), jnp.uint32),
    )
    tok_t = lambda b, s: (0, b * tiles + s)
    in_specs = [
        pl.BlockSpec((None, tm, d), lambda b, s: (b, s, 0)),
        pl.BlockSpec((None, 1, 6 * d), lambda b, s: (b, 0, 0)),
        cspec((1, d)),
        cspec(win_bf.shape),
        cspec(conv_w.shape),
        cspec(wsp.shape),
        cspec(bsp_full.shape),
        cspec((1, d)),
        cspec(wout_bf.shape),
        cspec((1, d)),
        cspec(rwt.shape),
        cspec(rb.shape),
    ]
    args = [x, mod_l, gmix, win_bf, conv_w, wsp, bsp_full, lng, wout_bf, gffn, rwt, rb]
    scratch = [
        pltpu.VMEM((tm, d), jnp.bfloat16),
        pltpu.VMEM((tm, 4 * d), jnp.float32),
        pltpu.VMEM((V7X_SUBLANES, d), jnp.float32),
        pltpu.VMEM((V7X_SUBLANES, d), jnp.float32),
        pltpu.VMEM((tm, d), jnp.float32),
        pltpu.VMEM((tm, d), jnp.bfloat16),
        pltpu.VMEM((n_exp, V7X_LANES), jnp.float32),
        pltpu.VMEM((2, tm * PIECES, V7X_LANES), jnp.uint32),
        pltpu.VMEM((TOP_K, tm), jnp.int32),
        pltpu.SMEM((TOP_K, tm), jnp.int32),
        pltpu.SemaphoreType.DMA(()),
        pltpu.SemaphoreType.DMA((2,)),
    ]
    if fused:
        pdest, pw, pmod, y = prev
        pdest_flat = _tile_major(pdest, tm)
        in_specs += [
            pl.BlockSpec((TOP_K * tm,), lambda b, s: (b * tiles + s,), memory_space=pltpu.SMEM),
            pl.BlockSpec((TOP_K * tm,),
                         lambda b, s: (jnp.minimum(b * tiles + s + 1, n_steps - 1),),
                         memory_space=pltpu.SMEM),
            pl.BlockSpec((tm, TOP_K), lambda b, s: (b * tiles + s, 0)),
            pl.BlockSpec((None, 1, 6 * d), lambda b, s: (b, 0, 0)),
            pl.BlockSpec(memory_space=pl.ANY),
        ]
        args += [pdest_flat, pdest_flat, pw.T, pmod, y]
        scratch += [
            pltpu.VMEM((tm, d), jnp.float32),
            pltpu.VMEM((2, TOP_K, tm * PIECES, V7X_LANES), jnp.uint32),
            pltpu.SemaphoreType.DMA((2,)),
        ]
    return pl.pallas_call(
        functools.partial(_mixer_kernel, capacity=n_tok, fused=fused),
        grid=(bsz, tiles),
        in_specs=in_specs,
        out_specs=(
            pl.BlockSpec((None, tm, d), lambda b, s: (b, s, 0)),
            pl.BlockSpec((TOP_K, tm), tok_t),
            pl.BlockSpec((TOP_K, tm), tok_t),
            pl.BlockSpec((n_exp, V7X_LANES), lambda b, s: (0, 0)),
            pl.BlockSpec(memory_space=pl.ANY),
        ),
        out_shape=out_shape,
        scratch_shapes=scratch,
        compiler_params=pltpu.CompilerParams(
            dimension_semantics=("arbitrary", "arbitrary"),
            vmem_limit_bytes=V7X_VMEM_LIMIT_BYTES),
        name="combine_mixer_router_dispatch" if fused else "mixer_router_dispatch",
    )(*args)


def _experts_kernel(blk_ref, exp_ref, nv_ref,
                    xs_ref, wgu_ref, bgu_ref, wdn_ref, bdn_ref, y_ref, wgu_bf, wdn_bf):
    i = pl.program_id(0)
    rows = xs_ref.shape[0] // PIECES
    half = PIECES * V7X_LANES
    d_ff = wdn_ref.shape[0]
    prev = jnp.maximum(i - 1, 0)
    new_expert = jnp.logical_or(i == 0, exp_ref[i] != exp_ref[prev])
    n_valid = nv_ref[i]

    @pl.when(new_expert)
    def _():
        wgu_bf[...] = wgu_ref[...].astype(jnp.bfloat16)
        wdn_bf[...] = wdn_ref[...].astype(jnp.bfloat16)

    @pl.when(n_valid > 0)
    def _():
        ridx = lax.broadcasted_iota(jnp.int32, (rows, half), 0)
        packed = jnp.where(ridx < n_valid, _load_rows(xs_ref, rows), jnp.uint32(0))
        x_hi, x_lo = _unpack_rows(packed)
        xs = jnp.concatenate([x_hi.astype(jnp.bfloat16), x_lo.astype(jnp.bfloat16)], axis=1)
        gu = jnp.dot(xs, wgu_bf[...], preferred_element_type=jnp.float32) + bgu_ref[...]
        gate = jnp.minimum(gu[:, :d_ff], SWIGLU_LIMIT)
        up = jnp.clip(gu[:, d_ff:], -SWIGLU_LIMIT, SWIGLU_LIMIT)
        act = gate * _sigmoid(SWIGLU_ALPHA * gate) * (up + 1.0)
        y = jnp.dot(act.astype(jnp.bfloat16), wdn_bf[...],
                    preferred_element_type=jnp.float32) + bdn_ref[...]
        _store_rows(y_ref, _pack_rows(y))


def _experts(items, xs, w_gu, b_gu, w_dn, b_dn, layer):
    n_rows = xs.shape[0] // PIECES
    half = PIECES * V7X_LANES
    d = 2 * half
    blk, exp, n_valid = items
    n_items = blk.shape[0]
    rows = EXPERT_BLOCK
    d_ff2 = w_gu.shape[-1]
    d_ff = w_dn.shape[-2]
    n_exp = w_gu.shape[1]
    depth = w_gu.shape[0]
    wmap = lambda i, blk, exp, nv: (layer, exp[i], 0, 0)
    grid_spec = pltpu.PrefetchScalarGridSpec(
        num_scalar_prefetch=3,
        grid=(n_items,),
        in_specs=[
            pl.BlockSpec((rows * PIECES, V7X_LANES), lambda i, blk, exp, nv: (blk[i], 0)),
            pl.BlockSpec((None, None, d, d_ff2), wmap),
            pl.BlockSpec((None, None, 1, d_ff2), wmap),
            pl.BlockSpec((None, None, d_ff, d), wmap),
            pl.BlockSpec((None, None, 1, d), wmap),
        ],
        out_specs=pl.BlockSpec((rows * PIECES, V7X_LANES), lambda i, blk, exp, nv: (blk[i], 0)),
        scratch_shapes=[
            pltpu.VMEM((d, d_ff2), jnp.bfloat16),
            pltpu.VMEM((d_ff, d), jnp.bfloat16),
        ],
    )
    return pl.pallas_call(
        _experts_kernel,
        grid_spec=grid_spec,
        out_shape=jax.ShapeDtypeStruct((n_rows * PIECES, V7X_LANES), jnp.uint32),
        compiler_params=pltpu.CompilerParams(
            dimension_semantics=("arbitrary",),
            vmem_limit_bytes=V7X_VMEM_LIMIT_BYTES),
        name="experts",
    )(blk, exp, n_valid, xs, w_gu, b_gu.reshape(depth, n_exp, 1, d_ff2), w_dn,
      b_dn.reshape(depth, n_exp, 1, d))


def _expert_items(counts, capacity, n_assign):
    n_exp = counts.shape[0]
    rows = EXPERT_BLOCK
    n_items = n_assign // rows + n_exp
    per_exp = (counts + rows - 1) // rows
    item_end = jnp.cumsum(per_exp)
    item_start = item_end - per_exp
    total = item_end[-1]
    idx = jnp.arange(n_items, dtype=jnp.int32)
    pick = jnp.minimum(idx, total - 1)
    exp = jnp.sum((item_end[None, :] <= pick[:, None]).astype(jnp.int32), axis=1)
    sub = pick - item_start[exp]
    blk = exp * (capacity // rows) + sub
    n_valid = jnp.where(idx < total, jnp.minimum(counts[exp] - sub * rows, rows), 0)
    return blk.astype(jnp.int32), exp.astype(jnp.int32), n_valid.astype(jnp.int32)


def _combine_kernel(dest_ref, x1_ref, w_ref, mod_ref, fin_ref, y_ref, o_ref, gbuf, sem):
    tmd, d = x1_ref.shape
    half = d // 2

    def issue(t, carry):
        for k in range(TOP_K):
            pltpu.make_async_copy(y_ref.at[pl.ds(dest_ref[k * tmd + t], PIECES)],
                                  gbuf.at[k, pl.ds(t * PIECES, PIECES)], sem).start()
        return carry

    lax.fori_loop(0, tmd, issue, 0)
    for k in range(TOP_K):
        pltpu.make_async_copy(y_ref.at[pl.ds(0, tmd * PIECES)], gbuf.at[k], sem).wait()

    w = w_ref[...]
    acc_hi, acc_lo = None, None
    for k in range(TOP_K):
        y_hi, y_lo = _unpack_rows(_load_rows(gbuf.at[k], tmd))
        wk = w[:, k:k + 1]
        acc_hi = wk * y_hi if acc_hi is None else acc_hi + wk * y_hi
        acc_lo = wk * y_lo if acc_lo is None else acc_lo + wk * y_lo
    g_f = mod_ref[:, 5 * d:6 * d]
    x_hi = x1_ref[:, 0:half] + g_f[:, 0:half] * acc_hi
    x_lo = x1_ref[:, half:d] + g_f[:, half:d] * acc_lo
    ms = (jnp.sum(x_hi * x_hi, axis=-1, keepdims=True)
          + jnp.sum(x_lo * x_lo, axis=-1, keepdims=True)) * (1.0 / d)
    r = lax.rsqrt(ms + NORM_EPS)
    o_ref[:, 0:half] = x_hi * r * fin_ref[:, 0:half]
    o_ref[:, half:d] = x_lo * r * fin_ref[:, half:d]


def _combine(dest, x1, top_w, mod_l, fin_g, y):
    bsz, seq, d = x1.shape
    tmd = COMBINE_TILE
    tiles = seq // tmd
    return pl.pallas_call(
        _combine_kernel,
        grid=(bsz, tiles),
        in_specs=[
            pl.BlockSpec((tmd * TOP_K,), lambda b, s: (b * tiles + s,), memory_space=pltpu.SMEM),
            pl.BlockSpec((None, tmd, d), lambda b, s: (b, s, 0)),
            pl.BlockSpec((tmd, TOP_K), lambda b, s: (b * tiles + s, 0)),
            pl.BlockSpec((None, 1, 6 * d), lambda b, s: (b, 0, 0)),
            pl.BlockSpec((1, d), lambda b, s: (0, 0)),
            pl.BlockSpec(memory_space=pl.ANY),
        ],
        out_specs=pl.BlockSpec((None, tmd, d), lambda b, s: (b, s, 0)),
        out_shape=jax.ShapeDtypeStruct((bsz, seq, d), jnp.float32),
        scratch_shapes=[
            pltpu.VMEM((TOP_K, tmd * PIECES, V7X_LANES), jnp.uint32),
            pltpu.SemaphoreType.DMA(()),
        ],
        compiler_params=pltpu.CompilerParams(dimension_semantics=("arbitrary", "arbitrary")),
        name="final_combine",
    )(_tile_major(dest, tmd), x1, top_w.T, mod_l, fin_g, y)


def kernel(x, c, w_ada, b_ada, norm_mix_g, w_in, conv_w, w_spatial, b_spatial, ln_v_g, w_out,
           norm_ffn_g, router_w, router_b, w_gate_up, b_gate_up, w_down, b_down, final_norm_g):
    bsz, seq, d = x.shape
    depth = w_ada.shape[0]
    n_exp = router_w.shape[-1]
    n_tok = bsz * seq
    assert seq % MIX_TILE == 0 and MIX_TILE % CHUNK == 0
    assert seq % COMBINE_TILE == 0 and n_tok % EXPERT_BLOCK == 0
    assert d // 2 == PIECES * V7X_LANES

    mod = _modulation(c, w_ada, b_ada)
    win_bf = w_in.astype(jnp.bfloat16)
    wout_bf = w_out.astype(jnp.bfloat16)
    bsp_full = jnp.repeat(jnp.swapaxes(b_spatial, 1, 2), d // GROUPS, axis=2)
    rwt = jnp.swapaxes(router_w, 1, 2)
    fin_g = final_norm_g.reshape(1, d)

    prev = None
    for l in range(depth):
        x, dest, top_w, cnt, xs = _mixer(
            x, mod[l], norm_mix_g[l].reshape(1, d), win_bf[l], conv_w[l], w_spatial[l],
            bsp_full[l], ln_v_g[l].reshape(1, d), wout_bf[l], norm_ffn_g[l].reshape(1, d),
            rwt[l], router_b[l].reshape(n_exp, 1), prev)
        items = _expert_items(cnt[:, 0], n_tok, n_tok * TOP_K)
        y = _experts(items, xs, w_gate_up, b_gate_up, w_down, b_down, l)
        prev = (dest, top_w, mod[l], y)
    return _combine(dest, x, top_w, mod[depth - 1], fin_g, y)
```

```python
import functools
import math

import jax
import jax.numpy as jnp
from jax import lax
from jax.experimental import pallas as pl
from jax.experimental.pallas import tpu as pltpu

CONV_K = 3
GROUPS = 8
CHUNK = 128
TOP_K = 4
SWIGLU_LIMIT = 7.0
SWIGLU_ALPHA = 1.702
NORM_EPS = 1e-6

V7X_SUBLANES = 8
V7X_LANES = 128
V7X_VMEM_LIMIT_BYTES = 56 * 1024 * 1024

MIX_TILE = 512
ROW_CHUNK = CHUNK
COMBINE_TILE = 256
EXPERT_BLOCK = 512
PIECES = 4


def _split_bf16(a):
    hi = a.astype(jnp.bfloat16)
    lo = (a - hi.astype(jnp.float32)).astype(jnp.bfloat16)
    return hi, lo


def _gelu(a):
    return 0.5 * a * (1.0 + lax.erf(a * (1.0 / math.sqrt(2.0))))


def _sigmoid(a):
    return 1.0 / (1.0 + jnp.exp(-a))


def _pack_rows(a):
    half = a.shape[1] // 2

    def bits(v):
        return lax.bitcast_convert_type(v.astype(jnp.bfloat16).astype(jnp.float32), jnp.uint32)

    return bits(a[:, :half]) | (bits(a[:, half:]) >> 16)


def _unpack_rows(p):
    hi = lax.bitcast_convert_type(p & jnp.uint32(0xFFFF0000), jnp.float32)
    lo = lax.bitcast_convert_type(p << 16, jnp.float32)
    return hi, lo


def _store_rows(ref, packed):
    rows = packed.shape[0]
    for c in range(PIECES):
        ref[pl.ds(c, rows, stride=PIECES), :] = packed[:, c * V7X_LANES:(c + 1) * V7X_LANES]


def _load_rows(ref, rows):
    return jnp.concatenate([ref[pl.ds(c, rows, stride=PIECES), :] for c in range(PIECES)], axis=1)


def _mod_kernel(c_ref, w_ref, b_ref, o_ref):
    c = c_ref[...]
    s = c * _sigmoid(c)
    s_hi, s_lo = _split_bf16(s)
    w_hi, w_lo = _split_bf16(w_ref[...])
    dot = functools.partial(jnp.dot, preferred_element_type=jnp.float32)
    o_ref[...] = dot(s_hi, w_hi) + dot(s_lo, w_hi) + dot(s_hi, w_lo) + b_ref[...]


def _modulation(c, w_ada, b_ada):
    depth, d, d6 = w_ada.shape
    bsz = c.shape[0]
    rows = -(-bsz // V7X_SUBLANES) * V7X_SUBLANES
    c_pad = jnp.zeros((rows, d), jnp.float32).at[:bsz].set(c)
    n_col = d6 // d
    out = pl.pallas_call(
        _mod_kernel,
        grid=(depth, n_col),
        in_specs=[
            pl.BlockSpec((rows, d), lambda l, j: (0, 0)),
            pl.BlockSpec((None, d, d), lambda l, j: (l, 0, j)),
            pl.BlockSpec((None, 1, d), lambda l, j: (l, 0, j)),
        ],
        out_specs=pl.BlockSpec((None, rows, d), lambda l, j: (l, 0, j)),
        out_shape=jax.ShapeDtypeStruct((depth, rows, d6), jnp.float32),
        name="adaln_modulation",
    )(c_pad, w_ada, b_ada.reshape(depth, 1, d6))
    return out[:, :bsz].reshape(depth, bsz, 1, d6)


def _mixer_kernel(*refs, capacity, fused):
    if fused:
        (x_ref, mod_ref, gmix_ref, win_ref, conv_ref, wsp_ref, bsp_ref, lng_ref, wout_ref,
         gffn_ref, rwt_ref, rb_ref, pdest_cur, pdest_nxt, pw_ref, pmod_ref, y_ref,
         x1_ref, dest_ref, tw_ref, cnt_ref, xs_ref,
         hbuf, zbuf, halo_x, halo_c, mabuf, mbuf, run_ref, h2buf, dest_v, dest_s,
         dest_sem, row_sem, xbuf, gbuf, gat_sem) = refs
    else:
        (x_ref, mod_ref, gmix_ref, win_ref, conv_ref, wsp_ref, bsp_ref, lng_ref, wout_ref,
         gffn_ref, rwt_ref, rb_ref,
         x1_ref, dest_ref, tw_ref, cnt_ref, xs_ref,
         hbuf, zbuf, halo_x, halo_c, mabuf, mbuf, run_ref, h2buf, dest_v, dest_s,
         dest_sem, row_sem) = refs
    tm, d = x_ref.shape
    half = d // 2
    n_exp = rwt_ref.shape[0]
    n_chunks = tm // ROW_CHUNK
    step = pl.program_id(0) * pl.num_programs(1) + pl.program_id(1)
    n_steps = pl.num_programs(0) * pl.num_programs(1)
    slot = step % 2
    dot = functools.partial(jnp.dot, preferred_element_type=jnp.float32)

    @pl.when(pl.program_id(1) == 0)
    def _():
        halo_x[...] = jnp.zeros_like(halo_x)
        halo_c[...] = jnp.zeros_like(halo_c)

    def row_copy(which, t, k, row):
        return pltpu.make_async_copy(h2buf.at[which, pl.ds(t * PIECES, PIECES)],
                                     xs_ref.at[pl.ds(row, PIECES)], row_sem.at[which])

    def tile_rows_wait(which):
        for _ in range(TOP_K):
            pltpu.make_async_copy(h2buf.at[which], xs_ref.at[pl.ds(0, tm * PIECES)],
                                  row_sem.at[which]).wait()

    def dest_to_smem():
        dest_copy = pltpu.make_async_copy(dest_v, dest_s, dest_sem)
        dest_copy.start()
        dest_copy.wait()

    def gather_copy(which, t, k, row):
        return pltpu.make_async_copy(y_ref.at[pl.ds(row, PIECES)],
                                     gbuf.at[which, k, pl.ds(t * PIECES, PIECES)],
                                     gat_sem.at[which])

    def tile_gather_wait(which):
        for k in range(TOP_K):
            pltpu.make_async_copy(y_ref.at[pl.ds(0, tm * PIECES)], gbuf.at[which, k],
                                  gat_sem.at[which]).wait()

    @pl.when(step == 0)
    def _():
        run_ref[...] = jnp.zeros_like(run_ref)
        h2buf[1] = jnp.zeros((tm * PIECES, V7X_LANES), jnp.uint32)
        spare = (n_exp * capacity + lax.broadcasted_iota(jnp.int32, (TOP_K, tm), 0) * tm
                 + lax.broadcasted_iota(jnp.int32, (TOP_K, tm), 1))
        dest_v[...] = spare * PIECES
        dest_to_smem()
        if fused:
            def first_gather(t, carry):
                for k in range(TOP_K):
                    gather_copy(0, t, k, pdest_cur[k * tm + t]).start()
                return carry

            lax.fori_loop(0, tm, first_gather, 0)

    def issue_rows(phase):
        if fused:
            for t in range(gather_rows[phase], gather_rows[phase + 1]):
                for k in range(TOP_K):
                    gather_copy(1 - slot, t, k, pdest_nxt[k * tm + t]).start()
        for t in range(dispatch_rows[phase], dispatch_rows[phase + 1]):
            for k in range(TOP_K):
                row_copy(1 - slot, t, k, dest_s[k, t]).start()

    if fused:
        gather_rows = (0, (tm * 14) // 16, tm, tm)
        dispatch_rows = (0, 0, (tm * 8) // 16, tm)
    else:
        dispatch_rows = (0, (tm * 7) // 16, (tm * 12) // 16, tm)

    mod = mod_ref[...]
    sh_m, sc_m, g_m = mod[:, 0:d], mod[:, d:2 * d], mod[:, 2 * d:3 * d]
    sh_f, sc_f = mod[:, 3 * d:4 * d], mod[:, 4 * d:5 * d]

    if fused:
        tile_gather_wait(slot)
        w = pw_ref[...]
        acc_hi, acc_lo = None, None
        for k in range(TOP_K):
            y_hi, y_lo = _unpack_rows(_load_rows(gbuf.at[slot, k], tm))
            wk = w[:, k:k + 1]
            acc_hi = wk * y_hi if acc_hi is None else acc_hi + wk * y_hi
            acc_lo = wk * y_lo if acc_lo is None else acc_lo + wk * y_lo
        g_prev = pmod_ref[:, 5 * d:6 * d]
        xbuf[:, 0:half] = x_ref[:, 0:half] + g_prev[:, 0:half] * acc_hi
        xbuf[:, half:d] = x_ref[:, half:d] + g_prev[:, half:d] * acc_lo
        xin = xbuf
    else:
        xin = x_ref

    x = xin[...]
    gs = gmix_ref[...] * (1.0 + sc_m)
    r = lax.rsqrt(jnp.mean(x * x, axis=-1, keepdims=True) + NORM_EPS)
    hbuf[...] = (x * r * gs + sh_m).astype(jnp.bfloat16)

    h = hbuf[...]
    issue_rows(0)
    zbuf[:, 0:3 * d] = dot(h, win_ref[:, 0:3 * d])
    zbuf[:, 3 * d:4 * d] = dot(h, win_ref[:, 5 * d:6 * d])

    cw = conv_ref[...]
    w0, w1, w2 = cw[0:1, :], cw[1:2, :], cw[2:3, :]
    row8 = lax.broadcasted_iota(jnp.int32, (V7X_SUBLANES, d), 0)

    def branch_a(i, carry):
        r0 = pl.multiple_of(i * ROW_CHUNK, ROW_CHUNK)
        rows = pl.ds(r0, ROW_CHUNK)
        p0 = zbuf[rows, 0:d] * zbuf[rows, 2 * d:3 * d]
        rp = pl.ds(pl.multiple_of(jnp.maximum(r0 - V7X_SUBLANES, 0), V7X_SUBLANES), V7X_SUBLANES)
        prev_in = zbuf[rp, 0:d] * zbuf[rp, 2 * d:3 * d]
        prev = jnp.where(i == 0, halo_x[...] * halo_c[...], prev_in)
        r1 = pltpu.roll(p0, 1, 0)
        r2 = pltpu.roll(p0, 2, 0)
        q1 = pltpu.roll(prev, 1, 0)
        q2 = pltpu.roll(prev, 2, 0)
        p1 = jnp.concatenate([jnp.where(row8 < 1, q1, r1[0:V7X_SUBLANES]), r1[V7X_SUBLANES:]], axis=0)
        p2 = jnp.concatenate([jnp.where(row8 < 2, q2, r2[0:V7X_SUBLANES]), r2[V7X_SUBLANES:]], axis=0)
        y_a = zbuf[rows, d:2 * d] * (w0 * p2 + w1 * p1 + w2 * p0)
        mabuf[rows, :] = _sigmoid(zbuf[rows, 3 * d:4 * d]) * y_a
        return carry

    lax.fori_loop(0, n_chunks, branch_a, 0)
    last8 = pl.ds(tm - V7X_SUBLANES, V7X_SUBLANES)
    halo_x[...] = zbuf[last8, 0:d]
    halo_c[...] = zbuf[last8, 2 * d:3 * d]

    issue_rows(1)
    zbuf[:, 0:2 * d] = dot(h, win_ref[:, 3 * d:5 * d])
    zbuf[:, 2 * d:3 * d] = dot(h, win_ref[:, 6 * d:7 * d])

    tri = (lax.broadcasted_iota(jnp.int32, (CHUNK, CHUNK), 1)
           <= lax.broadcasted_iota(jnp.int32, (CHUNK, CHUNK), 0))
    wm = [jnp.where(tri, wsp_ref[g], 0.0).astype(jnp.bfloat16) for g in range(GROUPS)]
    gd = d // GROUPS
    lng = lng_ref[...]

    def branch_b(i, carry):
        r0 = pl.multiple_of(i * ROW_CHUNK, ROW_CHUNK)
        rows = pl.ds(r0, ROW_CHUNK)
        gv = _gelu(zbuf[rows, d:2 * d])
        mu = jnp.mean(gv, axis=-1, keepdims=True)
        vc = gv - mu
        var = jnp.mean(vc * vc, axis=-1, keepdims=True)
        vn = (vc * lax.rsqrt(var + NORM_EPS) * lng).astype(jnp.bfloat16)
        s = jnp.concatenate(
            [dot(wm[g], vn[:, g * gd:(g + 1) * gd]) for g in range(GROUPS)], axis=1)
        y_b = _gelu(zbuf[rows, 0:d]) * (s + bsp_ref[...])
        m = mabuf[rows, :] + _sigmoid(zbuf[rows, 2 * d:3 * d]) * y_b
        mbuf[rows, :] = m.astype(jnp.bfloat16)
        return carry

    lax.fori_loop(0, n_chunks, branch_b, 0)

    @pl.when(step > 0)
    def _():
        tile_rows_wait(slot)

    issue_rows(2)
    x1 = xin[...] + g_m * dot(mbuf[...], wout_ref[...])
    x1_ref[...] = x1
    gs2 = gffn_ref[...] * (1.0 + sc_f)
    r2n = lax.rsqrt(jnp.mean(x1 * x1, axis=-1, keepdims=True) + NORM_EPS)
    h2 = x1 * r2n * gs2 + sh_f

    nt = (((1,), (1,)), ((), ()))
    dg = functools.partial(lax.dot_general, dimension_numbers=nt,
                           preferred_element_type=jnp.float32)
    h_hi, h_lo = _split_bf16(h2)
    r_hi, r_lo = _split_bf16(rwt_ref[...])
    logits = dg(r_hi, h_hi) + dg(r_hi, h_lo) + dg(r_lo, h_hi) + rb_ref[...]

    iota_e = lax.broadcasted_iota(jnp.int32, (n_exp, tm), 0)
    vals = logits
    tops, idxs, hots = [], [], []
    for _ in range(TOP_K):
        mx = jnp.max(vals, axis=0, keepdims=True)
        ix = jnp.min(jnp.where(vals == mx, iota_e, n_exp), axis=0, keepdims=True)
        hot = iota_e == ix
        vals = jnp.where(hot, -jnp.inf, vals)
        tops.append(mx)
        idxs.append(ix)
        hots.append(hot)
    exps = [jnp.exp(t - tops[0]) for t in tops]
    denom = exps[0] + exps[1] + exps[2] + exps[3]
    for k in range(TOP_K):
        tw_ref[k:k + 1, :] = exps[k] / denom

    member = jnp.where(hots[0] | hots[1] | hots[2] | hots[3], 1.0, 0.0)
    upper = (lax.broadcasted_iota(jnp.int32, (tm, tm), 0)
             < lax.broadcasted_iota(jnp.int32, (tm, tm), 1))
    before = dot(member.astype(jnp.bfloat16), jnp.where(upper, 1.0, 0.0).astype(jnp.bfloat16))
    run = run_ref[...]
    pos = before + run[:, 0:1]
    for k in range(TOP_K):
        rk = jnp.sum(jnp.where(hots[k], pos, 0.0), axis=0, keepdims=True)
        dest_k = (idxs[k] * capacity + rk.astype(jnp.int32)) * PIECES
        dest_ref[k:k + 1, :] = dest_k
        dest_v[k:k + 1, :] = dest_k
    run = run + jnp.sum(member, axis=1, keepdims=True)
    run_ref[...] = run
    cnt_ref[...] = run.astype(jnp.int32)

    dest_copy = pltpu.make_async_copy(dest_v, dest_s, dest_sem)
    dest_copy.start()
    _store_rows(h2buf.at[slot], _pack_rows(h2))
    dest_copy.wait()

    @pl.when(step == n_steps - 1)
    def _():
        def issue(t, carry):
            for k in range(TOP_K):
                row_copy(slot, t, k, dest_s[k, t]).start()
            return carry

        lax.fori_loop(0, tm, issue, 0)
        tile_rows_wait(1 - slot)
        tile_rows_wait(slot)
        if fused:
            tile_gather_wait(1 - slot)


def _tile_major(a, tm):
    n_tok = a.shape[1]
    return a.reshape(TOP_K, n_tok // tm, tm).transpose(1, 0, 2).reshape(-1)


def _mixer(x, mod_l, gmix, win_bf, conv_w, wsp, bsp_full, lng, wout_bf, gffn, rwt, rb, prev):
    bsz, seq, d = x.shape
    n_exp = rwt.shape[0]
    tm = MIX_TILE
    n_tok = bsz * seq
    tiles = seq // tm
    n_steps = bsz * tiles
    fused = prev is not None
    const = pl.Buffered(1)

    def cspec(shape):
        return pl.BlockSpec(shape, lambda b, s: (0,) * len(shape), pipeline_mode=const)

    out_shape = (
        jax.ShapeDtypeStruct((bsz, seq, d), jnp.float32),
        jax.ShapeDtypeStruct((TOP_K, n_tok), jnp.int32),
        jax.ShapeDtypeStruct((TOP_K, n_tok), jnp.float32),
        jax.ShapeDtypeStruct((n_exp, V7X_LANES), jnp.int32),
        jax.ShapeDtypeStruct(((n_exp * n_tok + TOP_K * tm) * PIECES, V7X_LANES), jnp.uint32),
    )
    tok_t = lambda b, s: (0, b * tiles + s)
    in_specs = [
        pl.BlockSpec((None, tm, d), lambda b, s: (b, s, 0)),
        pl.BlockSpec((None, 1, 6 * d), lambda b, s: (b, 0, 0)),
        cspec((1, d)),
        cspec(win_bf.shape),
        cspec(conv_w.shape),
        cspec(wsp.shape),
        cspec(bsp_full.shape),
        cspec((1, d)),
        cspec(wout_bf.shape),
        cspec((1, d)),
        cspec(rwt.shape),
        cspec(rb.shape),
    ]
    args = [x, mod_l, gmix, win_bf, conv_w, wsp, bsp_full, lng, wout_bf, gffn, rwt, rb]
    scratch = [
        pltpu.VMEM((tm, d), jnp.bfloat16),
        pltpu.VMEM((tm, 4 * d), jnp.float32),
        pltpu.VMEM((V7X_SUBLANES, d), jnp.float32),
        pltpu.VMEM((V7X_SUBLANES, d), jnp.float32),
        pltpu.VMEM((tm, d), jnp.float32),
        pltpu.VMEM((tm, d), jnp.bfloat16),
        pltpu.VMEM((n_exp, V7X_LANES), jnp.float32),
        pltpu.VMEM((2, tm * PIECES, V7X_LANES), jnp.uint32),
        pltpu.VMEM((TOP_K, tm), jnp.int32),
        pltpu.SMEM((TOP_K, tm), jnp.int32),
        pltpu.SemaphoreType.DMA(()),
        pltpu.SemaphoreType.DMA((2,)),
    ]
    if fused:
        pdest, pw, pmod, y = prev
        pdest_flat = _tile_major(pdest, tm)
        in_specs += [
            pl.BlockSpec((TOP_K * tm,), lambda b, s: (b * tiles + s,), memory_space=pltpu.SMEM),
            pl.BlockSpec((TOP_K * tm,),
                         lambda b, s: (jnp.minimum(b * tiles + s + 1, n_steps - 1),),
                         memory_space=pltpu.SMEM),
            pl.BlockSpec((tm, TOP_K), lambda b, s: (b * tiles + s, 0)),
            pl.BlockSpec((None, 1, 6 * d), lambda b, s: (b, 0, 0)),
            pl.BlockSpec(memory_space=pl.ANY),
        ]
        args += [pdest_flat, pdest_flat, pw.T, pmod, y]
        scratch += [
            pltpu.VMEM((tm, d), jnp.float32),
            pltpu.VMEM((2, TOP_K, tm * PIECES, V7X_LANES), jnp.uint32),
            pltpu.SemaphoreType.DMA((2,)),
        ]
    return pl.pallas_call(
        functools.partial(_mixer_kernel, capacity=n_tok, fused=fused),
        grid=(bsz, tiles),
        in_specs=in_specs,
        out_specs=(
            pl.BlockSpec((None, tm, d), lambda b, s: (b, s, 0)),
            pl.BlockSpec((TOP_K, tm), tok_t),
            pl.BlockSpec((TOP_K, tm), tok_t),
            pl.BlockSpec((n_exp, V7X_LANES), lambda b, s: (0, 0)),
            pl.BlockSpec(memory_space=pl.ANY),
        ),
        out_shape=out_shape,
        scratch_shapes=scratch,
        compiler_params=pltpu.CompilerParams(
            dimension_semantics=("arbitrary", "arbitrary"),
            vmem_limit_bytes=V7X_VMEM_LIMIT_BYTES),
        name="combine_mixer_router_dispatch" if fused else "mixer_router_dispatch",
    )(*args)


def _experts_kernel(blk_ref, exp_ref, nv_ref,
                    xs_ref, wgu_ref, bgu_ref, wdn_ref, bdn_ref, y_ref, wgu_bf, wdn_bf):
    i = pl.program_id(0)
    rows = xs_ref.shape[0] // PIECES
    half = PIECES * V7X_LANES
    d_ff = wdn_ref.shape[0]
    prev = jnp.maximum(i - 1, 0)
    new_expert = jnp.logical_or(i == 0, exp_ref[i] != exp_ref[prev])
    n_valid = nv_ref[i]

    @pl.when(new_expert)
    def _():
        wgu_bf[...] = wgu_ref[...].astype(jnp.bfloat16)
        wdn_bf[...] = wdn_ref[...].astype(jnp.bfloat16)

    @pl.when(n_valid > 0)
    def _():
        ridx = lax.broadcasted_iota(jnp.int32, (rows, half), 0)
        packed = jnp.where(ridx < n_valid, _load_rows(xs_ref, rows), jnp.uint32(0))
        x_hi, x_lo = _unpack_rows(packed)
        xs = jnp.concatenate([x_hi.astype(jnp.bfloat16), x_lo.astype(jnp.bfloat16)], axis=1)
        gu = jnp.dot(xs, wgu_bf[...], preferred_element_type=jnp.float32) + bgu_ref[...]
        gate = jnp.minimum(gu[:, :d_ff], SWIGLU_LIMIT)
        up = jnp.clip(gu[:, d_ff:], -SWIGLU_LIMIT, SWIGLU_LIMIT)
        act = gate * _sigmoid(SWIGLU_ALPHA * gate) * (up + 1.0)
        y = jnp.dot(act.astype(jnp.bfloat16), wdn_bf[...],
                    preferred_element_type=jnp.float32) + bdn_ref[...]
        _store_rows(y_ref, _pack_rows(y))


def _experts(items, xs, w_gu, b_gu, w_dn, b_dn, layer):
    n_rows = xs.shape[0] // PIECES
    half = PIECES * V7X_LANES
    d = 2 * half
    blk, exp, n_valid = items
    n_items = blk.shape[0]
    rows = EXPERT_BLOCK
    d_ff2 = w_gu.shape[-1]
    d_ff = w_dn.shape[-2]
    n_exp = w_gu.shape[1]
    depth = w_gu.shape[0]
    wmap = lambda i, blk, exp, nv: (layer, exp[i], 0, 0)
    grid_spec = pltpu.PrefetchScalarGridSpec(
        num_scalar_prefetch=3,
        grid=(n_items,),
        in_specs=[
            pl.BlockSpec((rows * PIECES, V7X_LANES), lambda i, blk, exp, nv: (blk[i], 0)),
            pl.BlockSpec((None, None, d, d_ff2), wmap),
            pl.BlockSpec((None, None, 1, d_ff2), wmap),
            pl.BlockSpec((None, None, d_ff, d), wmap),
            pl.BlockSpec((None, None, 1, d), wmap),
        ],
        out_specs=pl.BlockSpec((rows * PIECES, V7X_LANES), lambda i, blk, exp, nv: (blk[i], 0)),
        scratch_shapes=[
            pltpu.VMEM((d, d_ff2), jnp.bfloat16),
            pltpu.VMEM((d_ff, d), jnp.bfloat16),
        ],
    )
    return pl.pallas_call(
        _experts_kernel,
        grid_spec=grid_spec,
        out_shape=jax.ShapeDtypeStruct((n_rows * PIECES, V7X_LANES), jnp.uint32),
        compiler_params=pltpu.CompilerParams(
            dimension_semantics=("arbitrary",),
            vmem_limit_bytes=V7X_VMEM_LIMIT_BYTES),
        name="experts",
    )(blk, exp, n_valid, xs, w_gu, b_gu.reshape(depth, n_exp, 1, d_ff2), w_dn,
      b_dn.reshape(depth, n_exp, 1, d))


def _expert_items(counts, capacity, n_assign):
    n_exp = counts.shape[0]
    rows = EXPERT_BLOCK
    n_items = n_assign // rows + n_exp
    per_exp = (counts + rows - 1) // rows
    item_end = jnp.cumsum(per_exp)
    item_start = item_end - per_exp
    total = item_end[-1]
    idx = jnp.arange(n_items, dtype=jnp.int32)
    pick = jnp.minimum(idx, total - 1)
    exp = jnp.sum((item_end[None, :] <= pick[:, None]).astype(jnp.int32), axis=1)
    sub = pick - item_start[exp]
    blk = exp * (capacity // rows) + sub
    n_valid = jnp.where(idx < total, jnp.minimum(counts[exp] - sub * rows, rows), 0)
    return blk.astype(jnp.int32), exp.astype(jnp.int32), n_valid.astype(jnp.int32)


def _combine_kernel(dest_ref, x1_ref, w_ref, mod_ref, fin_ref, y_ref, o_ref, gbuf, sem):
    tmd, d = x1_ref.shape
    half = d // 2

    def issue(t, carry):
        for k in range(TOP_K):
            pltpu.make_async_copy(y_ref.at[pl.ds(dest_ref[k * tmd + t], PIECES)],
                                  gbuf.at[k, pl.ds(t * PIECES, PIECES)], sem).start()
        return carry

    lax.fori_loop(0, tmd, issue, 0)
    for k in range(TOP_K):
        pltpu.make_async_copy(y_ref.at[pl.ds(0, tmd * PIECES)], gbuf.at[k], sem).wait()

    w = w_ref[...]
    acc_hi, acc_lo = None, None
    for k in range(TOP_K):
        y_hi, y_lo = _unpack_rows(_load_rows(gbuf.at[k], tmd))
        wk = w[:, k:k + 1]
        acc_hi = wk * y_hi if acc_hi is None else acc_hi + wk * y_hi
        acc_lo = wk * y_lo if acc_lo is None else acc_lo + wk * y_lo
    g_f = mod_ref[:, 5 * d:6 * d]
    x_hi = x1_ref[:, 0:half] + g_f[:, 0:half] * acc_hi
    x_lo = x1_ref[:, half:d] + g_f[:, half:d] * acc_lo
    ms = (jnp.sum(x_hi * x_hi, axis=-1, keepdims=True)
          + jnp.sum(x_lo * x_lo, axis=-1, keepdims=True)) * (1.0 / d)
    r = lax.rsqrt(ms + NORM_EPS)
    o_ref[:, 0:half] = x_hi * r * fin_ref[:, 0:half]
    o_ref[:, half:d] = x_lo * r * fin_ref[:, half:d]


def _combine(dest, x1, top_w, mod_l, fin_g, y):
    bsz, seq, d = x1.shape
    tmd = COMBINE_TILE
    tiles = seq // tmd
    return pl.pallas_call(
        _combine_kernel,
        grid=(bsz, tiles),
        in_specs=[
            pl.BlockSpec((tmd * TOP_K,), lambda b, s: (b * tiles + s,), memory_space=pltpu.SMEM),
            pl.BlockSpec((None, tmd, d), lambda b, s: (b, s, 0)),
            pl.BlockSpec((tmd, TOP_K), lambda b, s: (b * tiles + s, 0)),
            pl.BlockSpec((None, 1, 6 * d), lambda b, s: (b, 0, 0)),
            pl.BlockSpec((1, d), lambda b, s: (0, 0)),
            pl.BlockSpec(memory_space=pl.ANY),
        ],
        out_specs=pl.BlockSpec((None, tmd, d), lambda b, s: (b, s, 0)),
        out_shape=jax.ShapeDtypeStruct((bsz, seq, d), jnp.float32),
        scratch_shapes=[
            pltpu.VMEM((TOP_K, tmd * PIECES, V7X_LANES), jnp.uint32),
            pltpu.SemaphoreType.DMA(()),
        ],
        compiler_params=pltpu.CompilerParams(dimension_semantics=("arbitrary", "arbitrary")),
        name="final_combine",
    )(_tile_major(dest, tmd), x1, top_w.T, mod_l, fin_g, y)


def kernel(x, c, w_ada, b_ada, norm_mix_g, w_in, conv_w, w_spatial, b_spatial, ln_v_g, w_out,
           norm_ffn_g, router_w, router_b, w_gate_up, b_gate_up, w_down, b_down, final_norm_g):
    bsz, seq, d = x.shape
    depth = w_ada.shape[0]
    n_exp = router_w.shape[-1]
    n_tok = bsz * seq
    assert seq % MIX_TILE == 0 and MIX_TILE % CHUNK == 0
    assert seq % COMBINE_TILE == 0 and n_tok % EXPERT_BLOCK == 0
    assert d // 2 == PIECES * V7X_LANES

    mod = _modulation(c, w_ada, b_ada)
    win_bf = w_in.astype(jnp.bfloat16)
    wout_bf = w_out.astype(jnp.bfloat16)
    bsp_full = jnp.repeat(jnp.swapaxes(b_spatial, 1, 2), d // GROUPS, axis=2)
    rwt = jnp.swapaxes(router_w, 1, 2)
    fin_g = final_norm_g.reshape(1, d)

    prev = None
    for l in range(depth):
        x, dest, top_w, cnt, xs = _mixer(
            x, mod[l], norm_mix_g[l].reshape(1, d), win_bf[l], conv_w[l], w_spatial[l],
            bsp_full[l], ln_v_g[l].reshape(1, d), wout_bf[l], norm_ffn_g[l].reshape(1, d),
            rwt[l], router_b[l].reshape(n_exp, 1), prev)
        items = _expert_items(cnt[:, 0], n_tok, n_tok * TOP_K)
        y = _experts(items, xs, w_gate_up, b_gate_up, w_down, b_down, l)
        prev = (dest, top_w, mod[l], y)
    return _combine(dest, x, top_w, mod[depth - 1], fin_g, y)
```

```python
import functools
import math

import jax
import jax.numpy as jnp
from jax import lax
from jax.experimental import pallas as pl
from jax.experimental.pallas import tpu as pltpu

CONV_K = 3
GROUPS = 8
CHUNK = 128
TOP_K = 4
SWIGLU_LIMIT = 7.0
SWIGLU_ALPHA = 1.702
NORM_EPS = 1e-6

V7X_SUBLANES = 8
V7X_LANES = 128
V7X_VMEM_LIMIT_BYTES = 56 * 1024 * 1024

MIX_TILE = 512
ROW_CHUNK = CHUNK
COMBINE_TILE = 256
EXPERT_BLOCK = 512
PIECES = 4


def _split_bf16(a):
    hi = a.astype(jnp.bfloat16)
    lo = (a - hi.astype(jnp.float32)).astype(jnp.bfloat16)
    return hi, lo


def _gelu(a):
    return 0.5 * a * (1.0 + lax.erf(a * (1.0 / math.sqrt(2.0))))


def _sigmoid(a):
    return 1.0 / (1.0 + jnp.exp(-a))


def _pack_rows(a):
    half = a.shape[1] // 2

    def bits(v):
        return lax.bitcast_convert_type(v.astype(jnp.bfloat16).astype(jnp.float32), jnp.uint32)

    return bits(a[:, :half]) | (bits(a[:, half:]) >> 16)


def _unpack_rows(p):
    hi = lax.bitcast_convert_type(p & jnp.uint32(0xFFFF0000), jnp.float32)
    lo = lax.bitcast_convert_type(p << 16, jnp.float32)
    return hi, lo


def _store_rows(ref, packed):
    rows = packed.shape[0]
    for c in range(PIECES):
        ref[pl.ds(c, rows, stride=PIECES), :] = packed[:, c * V7X_LANES:(c + 1) * V7X_LANES]


def _load_rows(ref, rows):
    return jnp.concatenate([ref[pl.ds(c, rows, stride=PIECES), :] for c in range(PIECES)], axis=1)


def _mod_kernel(c_ref, w_ref, b_ref, o_ref):
    c = c_ref[...]
    s = c * _sigmoid(c)
    s_hi, s_lo = _split_bf16(s)
    w_hi, w_lo = _split_bf16(w_ref[...])
    dot = functools.partial(jnp.dot, preferred_element_type=jnp.float32)
    o_ref[...] = dot(s_hi, w_hi) + dot(s_lo, w_hi) + dot(s_hi, w_lo) + b_ref[...]


def _modulation(c, w_ada, b_ada):
    depth, d, d6 = w_ada.shape
    bsz = c.shape[0]
    rows = -(-bsz // V7X_SUBLANES) * V7X_SUBLANES
    c_pad = jnp.zeros((rows, d), jnp.float32).at[:bsz].set(c)
    n_col = d6 // d
    out = pl.pallas_call(
        _mod_kernel,
        grid=(depth, n_col),
        in_specs=[
            pl.BlockSpec((rows, d), lambda l, j: (0, 0)),
            pl.BlockSpec((None, d, d), lambda l, j: (l, 0, j)),
            pl.BlockSpec((None, 1, d), lambda l, j: (l, 0, j)),
        ],
        out_specs=pl.BlockSpec((None, rows, d), lambda l, j: (l, 0, j)),
        out_shape=jax.ShapeDtypeStruct((depth, rows, d6), jnp.float32),
        name="adaln_modulation",
    )(c_pad, w_ada, b_ada.reshape(depth, 1, d6))
    return out[:, :bsz].reshape(depth, bsz, 1, d6)


def _mixer_kernel(*refs, capacity, fused):
    if fused:
        (x_ref, mod_ref, gmix_ref, win_ref, conv_ref, wsp_ref, bsp_ref, lng_ref, wout_ref,
         gffn_ref, rwt_ref, rb_ref, pdest_cur, pdest_nxt, pw_ref, pmod_ref, y_ref,
         x1_ref, dest_ref, tw_ref, cnt_ref, xs_ref,
         hbuf, zbuf, halo_x, halo_c, mabuf, mbuf, run_ref, h2buf, dest_v, dest_s,
         dest_sem, row_sem, xbuf, gbuf, gat_sem) = refs
    else:
        (x_ref, mod_ref, gmix_ref, win_ref, conv_ref, wsp_ref, bsp_ref, lng_ref, wout_ref,
         gffn_ref, rwt_ref, rb_ref,
         x1_ref, dest_ref, tw_ref, cnt_ref, xs_ref,
         hbuf, zbuf, halo_x, halo_c, mabuf, mbuf, run_ref, h2buf, dest_v, dest_s,
         dest_sem, row_sem) = refs
    tm, d = x_ref.shape
    half = d // 2
    n_exp = rwt_ref.shape[0]
    n_chunks = tm // ROW_CHUNK
    step = pl.program_id(0) * pl.num_programs(1) + pl.program_id(1)
    n_steps = pl.num_programs(0) * pl.num_programs(1)
    slot = step % 2
    dot = functools.partial(jnp.dot, preferred_element_type=jnp.float32)

    @pl.when(pl.program_id(1) == 0)
    def _():
        halo_x[...] = jnp.zeros_like(halo_x)
        halo_c[...] = jnp.zeros_like(halo_c)

    def row_copy(which, t, k, row):
        return pltpu.make_async_copy(h2buf.at[which, pl.ds(t * PIECES, PIECES)],
                                     xs_ref.at[pl.ds(row, PIECES)], row_sem.at[which])

    def tile_rows_wait(which):
        for _ in range(TOP_K):
            pltpu.make_async_copy(h2buf.at[which], xs_ref.at[pl.ds(0, tm * PIECES)],
                                  row_sem.at[which]).wait()

    def dest_to_smem():
        dest_copy = pltpu.make_async_copy(dest_v, dest_s, dest_sem)
        dest_copy.start()
        dest_copy.wait()

    def gather_copy(which, t, k, row):
        return pltpu.make_async_copy(y_ref.at[pl.ds(row, PIECES)],
                                     gbuf.at[which, k, pl.ds(t * PIECES, PIECES)],
                                     gat_sem.at[which])

    def tile_gather_wait(which):
        for k in range(TOP_K):
            pltpu.make_async_copy(y_ref.at[pl.ds(0, tm * PIECES)], gbuf.at[which, k],
                                  gat_sem.at[which]).wait()

    @pl.when(step == 0)
    def _():
        run_ref[...] = jnp.zeros_like(run_ref)
        h2buf[1] = jnp.zeros((tm * PIECES, V7X_LANES), jnp.uint32)
        spare = (n_exp * capacity + lax.broadcasted_iota(jnp.int32, (TOP_K, tm), 0) * tm
                 + lax.broadcasted_iota(jnp.int32, (TOP_K, tm), 1))
        dest_v[...] = spare * PIECES
        dest_to_smem()
        if fused:
            def first_gather(t, carry):
                for k in range(TOP_K):
                    gather_copy(0, t, k, pdest_cur[k * tm + t]).start()
                return carry

            lax.fori_loop(0, tm, first_gather, 0)

    def issue_rows(phase):
        if fused:
            for t in range(gather_rows[phase], gather_rows[phase + 1]):
                for k in range(TOP_K):
                    gather_copy(1 - slot, t, k, pdest_nxt[k * tm + t]).start(priority=1)
        for t in range(dispatch_rows[phase], dispatch_rows[phase + 1]):
            for k in range(TOP_K):
                row_copy(1 - slot, t, k, dest_s[k, t]).start()

    if fused:
        gather_rows = (0, (tm * 14) // 16, tm, tm)
        dispatch_rows = (0, 0, (tm * 8) // 16, tm)
    else:
        dispatch_rows = (0, (tm * 7) // 16, (tm * 12) // 16, tm)

    mod = mod_ref[...]
    sh_m, sc_m, g_m = mod[:, 0:d], mod[:, d:2 * d], mod[:, 2 * d:3 * d]
    sh_f, sc_f = mod[:, 3 * d:4 * d], mod[:, 4 * d:5 * d]

    if fused:
        tile_gather_wait(slot)
        w = pw_ref[...]
        acc_hi, acc_lo = None, None
        for k in range(TOP_K):
            y_hi, y_lo = _unpack_rows(_load_rows(gbuf.at[slot, k], tm))
            wk = w[:, k:k + 1]
            acc_hi = wk * y_hi if acc_hi is None else acc_hi + wk * y_hi
            acc_lo = wk * y_lo if acc_lo is None else acc_lo + wk * y_lo
        g_prev = pmod_ref[:, 5 * d:6 * d]
        xbuf[:, 0:half] = x_ref[:, 0:half] + g_prev[:, 0:half] * acc_hi
        xbuf[:, half:d] = x_ref[:, half:d] + g_prev[:, half:d] * acc_lo
        xin = xbuf
    else:
        xin = x_ref

    x = xin[...]
    gs = gmix_ref[...] * (1.0 + sc_m)
    r = lax.rsqrt(jnp.mean(x * x, axis=-1, keepdims=True) + NORM_EPS)
    hbuf[...] = (x * r * gs + sh_m).astype(jnp.bfloat16)

    h = hbuf[...]
    issue_rows(0)
    zbuf[:, 0:3 * d] = dot(h, win_ref[:, 0:3 * d])
    zbuf[:, 3 * d:4 * d] = dot(h, win_ref[:, 5 * d:6 * d])

    cw = conv_ref[...]
    w0, w1, w2 = cw[0:1, :], cw[1:2, :], cw[2:3, :]
    row8 = lax.broadcasted_iota(jnp.int32, (V7X_SUBLANES, d), 0)

    def branch_a(i, carry):
        r0 = pl.multiple_of(i * ROW_CHUNK, ROW_CHUNK)
        rows = pl.ds(r0, ROW_CHUNK)
        p0 = zbuf[rows, 0:d] * zbuf[rows, 2 * d:3 * d]
        rp = pl.ds(pl.multiple_of(jnp.maximum(r0 - V7X_SUBLANES, 0), V7X_SUBLANES), V7X_SUBLANES)
        prev_in = zbuf[rp, 0:d] * zbuf[rp, 2 * d:3 * d]
        prev = jnp.where(i == 0, halo_x[...] * halo_c[...], prev_in)
        r1 = pltpu.roll(p0, 1, 0)
        r2 = pltpu.roll(p0, 2, 0)
        q1 = pltpu.roll(prev, 1, 0)
        q2 = pltpu.roll(prev, 2, 0)
        p1 = jnp.concatenate([jnp.where(row8 < 1, q1, r1[0:V7X_SUBLANES]), r1[V7X_SUBLANES:]], axis=0)
        p2 = jnp.concatenate([jnp.where(row8 < 2, q2, r2[0:V7X_SUBLANES]), r2[V7X_SUBLANES:]], axis=0)
        y_a = zbuf[rows, d:2 * d] * (w0 * p2 + w1 * p1 + w2 * p0)
        mabuf[rows, :] = _sigmoid(zbuf[rows, 3 * d:4 * d]) * y_a
        return carry

    lax.fori_loop(0, n_chunks, branch_a, 0)
    last8 = pl.ds(tm - V7X_SUBLANES, V7X_SUBLANES)
    halo_x[...] = zbuf[last8, 0:d]
    halo_c[...] = zbuf[last8, 2 * d:3 * d]

    issue_rows(1)
    zbuf[:, 0:2 * d] = dot(h, win_ref[:, 3 * d:5 * d])
    zbuf[:, 2 * d:3 * d] = dot(h, win_ref[:, 6 * d:7 * d])

    tri = (lax.broadcasted_iota(jnp.int32, (CHUNK, CHUNK), 1)
           <= lax.broadcasted_iota(jnp.int32, (CHUNK, CHUNK), 0))
    wm = [jnp.where(tri, wsp_ref[g], 0.0).astype(jnp.bfloat16) for g in range(GROUPS)]
    gd = d // GROUPS
    lng = lng_ref[...]

    def branch_b(i, carry):
        r0 = pl.multiple_of(i * ROW_CHUNK, ROW_CHUNK)
        rows = pl.ds(r0, ROW_CHUNK)
        gv = _gelu(zbuf[rows, d:2 * d])
        mu = jnp.mean(gv, axis=-1, keepdims=True)
        vc = gv - mu
        var = jnp.mean(vc * vc, axis=-1, keepdims=True)
        vn = (vc * lax.rsqrt(var + NORM_EPS) * lng).astype(jnp.bfloat16)
        s = jnp.concatenate(
            [dot(wm[g], vn[:, g * gd:(g + 1) * gd]) for g in range(GROUPS)], axis=1)
        y_b = _gelu(zbuf[rows, 0:d]) * (s + bsp_ref[...])
        m = mabuf[rows, :] + _sigmoid(zbuf[rows, 2 * d:3 * d]) * y_b
        mbuf[rows, :] = m.astype(jnp.bfloat16)
        return carry

    lax.fori_loop(0, n_chunks, branch_b, 0)

    @pl.when(step > 0)
    def _():
        tile_rows_wait(slot)

    issue_rows(2)
    x1 = xin[...] + g_m * dot(mbuf[...], wout_ref[...])
    x1_ref[...] = x1
    gs2 = gffn_ref[...] * (1.0 + sc_f)
    r2n = lax.rsqrt(jnp.mean(x1 * x1, axis=-1, keepdims=True) + NORM_EPS)
    h2 = x1 * r2n * gs2 + sh_f

    nt = (((1,), (1,)), ((), ()))
    dg = functools.partial(lax.dot_general, dimension_numbers=nt,
                           preferred_element_type=jnp.float32)
    h_hi, h_lo = _split_bf16(h2)
    r_hi, r_lo = _split_bf16(rwt_ref[...])
    logits = dg(r_hi, h_hi) + dg(r_hi, h_lo) + dg(r_lo, h_hi) + rb_ref[...]

    iota_e = lax.broadcasted_iota(jnp.int32, (n_exp, tm), 0)
    vals = logits
    tops, idxs, hots = [], [], []
    for _ in range(TOP_K):
        mx = jnp.max(vals, axis=0, keepdims=True)
        ix = jnp.min(jnp.where(vals == mx, iota_e, n_exp), axis=0, keepdims=True)
        hot = iota_e == ix
        vals = jnp.where(hot, -jnp.inf, vals)
        tops.append(mx)
        idxs.append(ix)
        hots.append(hot)
    exps = [jnp.exp(t - tops[0]) for t in tops]
    denom = exps[0] + exps[1] + exps[2] + exps[3]
    for k in range(TOP_K):
        tw_ref[k:k + 1, :] = exps[k] / denom

    member = jnp.where(hots[0] | hots[1] | hots[2] | hots[3], 1.0, 0.0)
    upper = (lax.broadcasted_iota(jnp.int32, (tm, tm), 0)
             < lax.broadcasted_iota(jnp.int32, (tm, tm), 1))
    before = dot(member.astype(jnp.bfloat16), jnp.where(upper, 1.0, 0.0).astype(jnp.bfloat16))
    run = run_ref[...]
    pos = before + run[:, 0:1]
    for k in range(TOP_K):
        rk = jnp.sum(jnp.where(hots[k], pos, 0.0), axis=0, keepdims=True)
        dest_k = (idxs[k] * capacity + rk.astype(jnp.int32)) * PIECES
        dest_ref[k:k + 1, :] = dest_k
        dest_v[k:k + 1, :] = dest_k
    run = run + jnp.sum(member, axis=1, keepdims=True)
    run_ref[...] = run
    cnt_ref[...] = run.astype(jnp.int32)

    dest_copy = pltpu.make_async_copy(dest_v, dest_s, dest_sem)
    dest_copy.start()
    _store_rows(h2buf.at[slot], _pack_rows(h2))
    dest_copy.wait()

    @pl.when(step == n_steps - 1)
    def _():
        def issue(t, carry):
            for k in range(TOP_K):
                row_copy(slot, t, k, dest_s[k, t]).start()
            return carry

        lax.fori_loop(0, tm, issue, 0)
        tile_rows_wait(1 - slot)
        tile_rows_wait(slot)
        if fused:
            tile_gather_wait(1 - slot)


def _tile_major(a, tm):
    n_tok = a.shape[1]
    return a.reshape(TOP_K, n_tok // tm, tm).transpose(1, 0, 2).reshape(-1)


def _mixer(x, mod_l, gmix, win_bf, conv_w, wsp, bsp_full, lng, wout_bf, gffn, rwt, rb, prev):
    bsz, seq, d = x.shape
    n_exp = rwt.shape[0]
    tm = MIX_TILE
    n_tok = bsz * seq
    tiles = seq // tm
    n_steps = bsz * tiles
    fused = prev is not None
    const = pl.Buffered(1)

    def cspec(shape):
        return pl.BlockSpec(shape, lambda b, s: (0,) * len(shape), pipeline_mode=const)

    out_shape = (
        jax.ShapeDtypeStruct((bsz, seq, d), jnp.float32),
        jax.ShapeDtypeStruct((TOP_K, n_tok), jnp.int32),
        jax.ShapeDtypeStruct((TOP_K, n_tok), jnp.float32),
        jax.ShapeDtypeStruct((n_exp, V7X_LANES), jnp.int32),
        jax.ShapeDtypeStruct(((n_exp * n_tok + TOP_K * tm) * PIECES, V7X_LANES), jnp.uint32),
    )
    tok_t = lambda b, s: (0, b * tiles + s)
    in_specs = [
        pl.BlockSpec((None, tm, d), lambda b, s: (b, s, 0)),
        pl.BlockSpec((None, 1, 6 * d), lambda b, s: (b, 0, 0)),
        cspec((1, d)),
        cspec(win_bf.shape),
        cspec(conv_w.shape),
        cspec(wsp.shape),
        cspec(bsp_full.shape),
        cspec((1, d)),
        cspec(wout_bf.shape),
        cspec((1, d)),
        cspec(rwt.shape),
        cspec(rb.shape),
    ]
    args = [x, mod_l, gmix, win_bf, conv_w, wsp, bsp_full, lng, wout_bf, gffn, rwt, rb]
    scratch = [
        pltpu.VMEM((tm, d), jnp.bfloat16),
        pltpu.VMEM((tm, 4 * d), jnp.float32),
        pltpu.VMEM((V7X_SUBLANES, d), jnp.float32),
        pltpu.VMEM((V7X_SUBLANES, d), jnp.float32),
        pltpu.VMEM((tm, d), jnp.float32),
        pltpu.VMEM((tm, d), jnp.bfloat16),
        pltpu.VMEM((n_exp, V7X_LANES), jnp.float32),
        pltpu.VMEM((2, tm * PIECES, V7X_LANES), jnp.uint32),
        pltpu.VMEM((TOP_K, tm), jnp.int32),
        pltpu.SMEM((TOP_K, tm), jnp.int32),
        pltpu.SemaphoreType.DMA(()),
        pltpu.SemaphoreType.DMA((2,)),
    ]
    if fused:
        pdest, pw, pmod, y = prev
        pdest_flat = _tile_major(pdest, tm)
        in_specs += [
            pl.BlockSpec((TOP_K * tm,), lambda b, s: (b * tiles + s,), memory_space=pltpu.SMEM),
            pl.BlockSpec((TOP_K * tm,),
                         lambda b, s: (jnp.minimum(b * tiles + s + 1, n_steps - 1),),
                         memory_space=pltpu.SMEM),
            pl.BlockSpec((tm, TOP_K), lambda b, s: (b * tiles + s, 0)),
            pl.BlockSpec((None, 1, 6 * d), lambda b, s: (b, 0, 0)),
            pl.BlockSpec(memory_space=pl.ANY),
        ]
        args += [pdest_flat, pdest_flat, pw.T, pmod, y]
        scratch += [
            pltpu.VMEM((tm, d), jnp.float32),
            pltpu.VMEM((2, TOP_K, tm * PIECES, V7X_LANES), jnp.uint32),
            pltpu.SemaphoreType.DMA((2,)),
        ]
    return pl.pallas_call(
        functools.partial(_mixer_kernel, capacity=n_tok, fused=fused),
        grid=(bsz, tiles),
        in_specs=in_specs,
        out_specs=(
            pl.BlockSpec((None, tm, d), lambda b, s: (b, s, 0)),
            pl.BlockSpec((TOP_K, tm), tok_t),
            pl.BlockSpec((TOP_K, tm), tok_t),
            pl.BlockSpec((n_exp, V7X_LANES), lambda b, s: (0, 0)),
            pl.BlockSpec(memory_space=pl.ANY),
        ),
        out_shape=out_shape,
        scratch_shapes=scratch,
        compiler_params=pltpu.CompilerParams(
            dimension_semantics=("arbitrary", "arbitrary"),
            vmem_limit_bytes=V7X_VMEM_LIMIT_BYTES),
        name="combine_mixer_router_dispatch" if fused else "mixer_router_dispatch",
    )(*args)


def _experts_kernel(blk_ref, exp_ref, nv_ref,
                    xs_ref, wgu_ref, bgu_ref, wdn_ref, bdn_ref, y_ref, wgu_bf, wdn_bf):
    i = pl.program_id(0)
    rows = xs_ref.shape[0] // PIECES
    half = PIECES * V7X_LANES
    d_ff = wdn_ref.shape[0]
    prev = jnp.maximum(i - 1, 0)
    new_expert = jnp.logical_or(i == 0, exp_ref[i] != exp_ref[prev])
    n_valid = nv_ref[i]

    @pl.when(new_expert)
    def _():
        wgu_bf[...] = wgu_ref[...].astype(jnp.bfloat16)
        wdn_bf[...] = wdn_ref[...].astype(jnp.bfloat16)

    @pl.when(n_valid > 0)
    def _():
        ridx = lax.broadcasted_iota(jnp.int32, (rows, half), 0)
        packed = jnp.where(ridx < n_valid, _load_rows(xs_ref, rows), jnp.uint32(0))
        x_hi, x_lo = _unpack_rows(packed)
        xs = jnp.concatenate([x_hi.astype(jnp.bfloat16), x_lo.astype(jnp.bfloat16)], axis=1)
        gu = jnp.dot(xs, wgu_bf[...], preferred_element_type=jnp.float32) + bgu_ref[...]
        gate = jnp.minimum(gu[:, :d_ff], SWIGLU_LIMIT)
        up = jnp.clip(gu[:, d_ff:], -SWIGLU_LIMIT, SWIGLU_LIMIT)
        act = gate * _sigmoid(SWIGLU_ALPHA * gate) * (up + 1.0)
        y = jnp.dot(act.astype(jnp.bfloat16), wdn_bf[...],
                    preferred_element_type=jnp.float32) + bdn_ref[...]
        _store_rows(y_ref, _pack_rows(y))


def _experts(items, xs, w_gu, b_gu, w_dn, b_dn, layer):
    n_rows = xs.shape[0] // PIECES
    half = PIECES * V7X_LANES
    d = 2 * half
    blk, exp, n_valid = items
    n_items = blk.shape[0]
    rows = EXPERT_BLOCK
    d_ff2 = w_gu.shape[-1]
    d_ff = w_dn.shape[-2]
    n_exp = w_gu.shape[1]
    depth = w_gu.shape[0]
    wmap = lambda i, blk, exp, nv: (layer, exp[i], 0, 0)
    grid_spec = pltpu.PrefetchScalarGridSpec(
        num_scalar_prefetch=3,
        grid=(n_items,),
        in_specs=[
            pl.BlockSpec((rows * PIECES, V7X_LANES), lambda i, blk, exp, nv: (blk[i], 0)),
            pl.BlockSpec((None, None, d, d_ff2), wmap),
            pl.BlockSpec((None, None, 1, d_ff2), wmap),
            pl.BlockSpec((None, None, d_ff, d), wmap),
            pl.BlockSpec((None, None, 1, d), wmap),
        ],
        out_specs=pl.BlockSpec((rows * PIECES, V7X_LANES), lambda i, blk, exp, nv: (blk[i], 0)),
        scratch_shapes=[
            pltpu.VMEM((d, d_ff2), jnp.bfloat16),
            pltpu.VMEM((d_ff, d), jnp.bfloat16),
        ],
    )
    return pl.pallas_call(
        _experts_kernel,
        grid_spec=grid_spec,
        out_shape=jax.ShapeDtypeStruct((n_rows * PIECES, V7X_LANES), jnp.uint32),
        compiler_params=pltpu.CompilerParams(
            dimension_semantics=("arbitrary",),
            vmem_limit_bytes=V7X_VMEM_LIMIT_BYTES),
        name="experts",
    )(blk, exp, n_valid, xs, w_gu, b_gu.reshape(depth, n_exp, 1, d_ff2), w_dn,
      b_dn.reshape(depth, n_exp, 1, d))


def _expert_items(counts, capacity, n_assign):
    n_exp = counts.shape[0]
    rows = EXPERT_BLOCK
    n_items = n_assign // rows + n_exp
    per_exp = (counts + rows - 1) // rows
    item_end = jnp.cumsum(per_exp)
    item_start = item_end - per_exp
    total = item_end[-1]
    idx = jnp.arange(n_items, dtype=jnp.int32)
    pick = jnp.minimum(idx, total - 1)
    exp = jnp.sum((item_end[None, :] <= pick[:, None]).astype(jnp.int32), axis=1)
    sub = pick - item_start[exp]
    blk = exp * (capacity // rows) + sub
    n_valid = jnp.where(idx < total, jnp.minimum(counts[exp] - sub * rows, rows), 0)
    return blk.astype(jnp.int32), exp.astype(jnp.int32), n_valid.astype(jnp.int32)


def _combine_kernel(dest_ref, x1_ref, w_ref, mod_ref, fin_ref, y_ref, o_ref, gbuf, sem):
    tmd, d = x1_ref.shape
    half = d // 2

    def issue(t, carry):
        for k in range(TOP_K):
            pltpu.make_async_copy(y_ref.at[pl.ds(dest_ref[k * tmd + t], PIECES)],
                                  gbuf.at[k, pl.ds(t * PIECES, PIECES)], sem).start()
        return carry

    lax.fori_loop(0, tmd, issue, 0)
    for k in range(TOP_K):
        pltpu.make_async_copy(y_ref.at[pl.ds(0, tmd * PIECES)], gbuf.at[k], sem).wait()

    w = w_ref[...]
    acc_hi, acc_lo = None, None
    for k in range(TOP_K):
        y_hi, y_lo = _unpack_rows(_load_rows(gbuf.at[k], tmd))
        wk = w[:, k:k + 1]
        acc_hi = wk * y_hi if acc_hi is None else acc_hi + wk * y_hi
        acc_lo = wk * y_lo if acc_lo is None else acc_lo + wk * y_lo
    g_f = mod_ref[:, 5 * d:6 * d]
    x_hi = x1_ref[:, 0:half] + g_f[:, 0:half] * acc_hi
    x_lo = x1_ref[:, half:d] + g_f[:, half:d] * acc_lo
    ms = (jnp.sum(x_hi * x_hi, axis=-1, keepdims=True)
          + jnp.sum(x_lo * x_lo, axis=-1, keepdims=True)) * (1.0 / d)
    r = lax.rsqrt(ms + NORM_EPS)
    o_ref[:, 0:half] = x_hi * r * fin_ref[:, 0:half]
    o_ref[:, half:d] = x_lo * r * fin_ref[:, half:d]


def _combine(dest, x1, top_w, mod_l, fin_g, y):
    bsz, seq, d = x1.shape
    tmd = COMBINE_TILE
    tiles = seq // tmd
    return pl.pallas_call(
        _combine_kernel,
        grid=(bsz, tiles),
        in_specs=[
            pl.BlockSpec((tmd * TOP_K,), lambda b, s: (b * tiles + s,), memory_space=pltpu.SMEM),
            pl.BlockSpec((None, tmd, d), lambda b, s: (b, s, 0)),
            pl.BlockSpec((tmd, TOP_K), lambda b, s: (b * tiles + s, 0)),
            pl.BlockSpec((None, 1, 6 * d), lambda b, s: (b, 0, 0)),
            pl.BlockSpec((1, d), lambda b, s: (0, 0)),
            pl.BlockSpec(memory_space=pl.ANY),
        ],
        out_specs=pl.BlockSpec((None, tmd, d), lambda b, s: (b, s, 0)),
        out_shape=jax.ShapeDtypeStruct((bsz, seq, d), jnp.float32),
        scratch_shapes=[
            pltpu.VMEM((TOP_K, tmd * PIECES, V7X_LANES), jnp.uint32),
            pltpu.SemaphoreType.DMA(()),
        ],
        compiler_params=pltpu.CompilerParams(dimension_semantics=("arbitrary", "arbitrary")),
        name="final_combine",
    )(_tile_major(dest, tmd), x1, top_w.T, mod_l, fin_g, y)


def kernel(x, c, w_ada, b_ada, norm_mix_g, w_in, conv_w, w_spatial, b_spatial, ln_v_g, w_out,
           norm_ffn_g, router_w, router_b, w_gate_up, b_gate_up, w_down, b_down, final_norm_g):
    bsz, seq, d = x.shape
    depth = w_ada.shape[0]
    n_exp = router_w.shape[-1]
    n_tok = bsz * seq
    assert seq % MIX_TILE == 0 and MIX_TILE % CHUNK == 0
    assert seq % COMBINE_TILE == 0 and n_tok % EXPERT_BLOCK == 0
    assert d // 2 == PIECES * V7X_LANES

    mod = _modulation(c, w_ada, b_ada)
    win_bf = w_in.astype(jnp.bfloat16)
    wout_bf = w_out.astype(jnp.bfloat16)
    bsp_full = jnp.repeat(jnp.swapaxes(b_spatial, 1, 2), d // GROUPS, axis=2)
    rwt = jnp.swapaxes(router_w, 1, 2)
    fin_g = final_norm_g.reshape(1, d)

    prev = None
    for l in range(depth):
        x, dest, top_w, cnt, xs = _mixer(
            x, mod[l], norm_mix_g[l].reshape(1, d), win_bf[l], conv_w[l], w_spatial[l],
            bsp_full[l], ln_v_g[l].reshape(1, d), wout_bf[l], norm_ffn_g[l].reshape(1, d),
            rwt[l], router_b[l].reshape(n_exp, 1), prev)
        items = _expert_items(cnt[:, 0], n_tok, n_tok * TOP_K)
        y = _experts(items, xs, w_gate_up, b_gate_up, w_down, b_down, l)
        prev = (dest, top_w, mod[l], y)
    return _combine(dest, x, top_w, mod[depth - 1], fin_g, y)
```

```python
import functools
import math

import jax
import jax.numpy as jnp
from jax import lax
from jax.experimental import pallas as pl
from jax.experimental.pallas import tpu as pltpu

CONV_K = 3
GROUPS = 8
CHUNK = 128
TOP_K = 4
SWIGLU_LIMIT = 7.0
SWIGLU_ALPHA = 1.702
NORM_EPS = 1e-6

V7X_SUBLANES = 8
V7X_LANES = 128
V7X_VMEM_LIMIT_BYTES = 56 * 1024 * 1024

MIX_TILE = 512
ROW_CHUNK = CHUNK
COMBINE_TILE = 256
EXPERT_BLOCK = 512
PIECES = 4
MATMUL_SECTIONS = 8


def _split_bf16(a):
    hi = a.astype(jnp.bfloat16)
    lo = (a - hi.astype(jnp.float32)).astype(jnp.bfloat16)
    return hi, lo


def _gelu(a):
    return 0.5 * a * (1.0 + lax.erf(a * (1.0 / math.sqrt(2.0))))


def _sigmoid(a):
    return 1.0 / (1.0 + jnp.exp(-a))


def _pack_rows(a):
    half = a.shape[1] // 2

    def bits(v):
        return lax.bitcast_convert_type(v.astype(jnp.bfloat16).astype(jnp.float32), jnp.uint32)

    return bits(a[:, :half]) | (bits(a[:, half:]) >> 16)


def _unpack_rows(p):
    hi = lax.bitcast_convert_type(p & jnp.uint32(0xFFFF0000), jnp.float32)
    lo = lax.bitcast_convert_type(p << 16, jnp.float32)
    return hi, lo


def _store_rows(ref, packed):
    rows = packed.shape[0]
    for c in range(PIECES):
        ref[pl.ds(c, rows, stride=PIECES), :] = packed[:, c * V7X_LANES:(c + 1) * V7X_LANES]


def _load_rows(ref, rows):
    return jnp.concatenate([ref[pl.ds(c, rows, stride=PIECES), :] for c in range(PIECES)], axis=1)


def _mod_kernel(c_ref, w_ref, b_ref, o_ref):
    c = c_ref[...]
    s = c * _sigmoid(c)
    s_hi, s_lo = _split_bf16(s)
    w_hi, w_lo = _split_bf16(w_ref[...])
    dot = functools.partial(jnp.dot, preferred_element_type=jnp.float32)
    o_ref[...] = dot(s_hi, w_hi) + dot(s_lo, w_hi) + dot(s_hi, w_lo) + b_ref[...]


def _modulation(c, w_ada, b_ada):
    depth, d, d6 = w_ada.shape
    bsz = c.shape[0]
    rows = -(-bsz // V7X_SUBLANES) * V7X_SUBLANES
    c_pad = jnp.zeros((rows, d), jnp.float32).at[:bsz].set(c)
    n_col = d6 // d
    out = pl.pallas_call(
        _mod_kernel,
        grid=(depth, n_col),
        in_specs=[
            pl.BlockSpec((rows, d), lambda l, j: (0, 0)),
            pl.BlockSpec((None, d, d), lambda l, j: (l, 0, j)),
            pl.BlockSpec((None, 1, d), lambda l, j: (l, 0, j)),
        ],
        out_specs=pl.BlockSpec((None, rows, d), lambda l, j: (l, 0, j)),
        out_shape=jax.ShapeDtypeStruct((depth, rows, d6), jnp.float32),
        name="adaln_modulation",
    )(c_pad, w_ada, b_ada.reshape(depth, 1, d6))
    return out[:, :bsz].reshape(depth, bsz, 1, d6)


def _mixer_kernel(*refs, capacity, fused):
    if fused:
        (x_ref, mod_ref, gmix_ref, win_ref, conv_ref, wsp_ref, bsp_ref, lng_ref, wout_ref,
         gffn_ref, rwt_ref, rb_ref, pdest_cur, pdest_nxt, pw_ref, pmod_ref, y_ref,
         x1_ref, dest_ref, tw_ref, cnt_ref, xs_ref,
         hbuf, zbuf, halo_x, halo_c, mabuf, mbuf, run_ref, h2buf, dest_v, dest_s,
         dest_sem, row_sem, xbuf, gbuf, gat_sem, wt_v) = refs
    else:
        (x_ref, mod_ref, gmix_ref, win_ref, conv_ref, wsp_ref, bsp_ref, lng_ref, wout_ref,
         gffn_ref, rwt_ref, rb_ref,
         x1_ref, dest_ref, tw_ref, cnt_ref, xs_ref,
         hbuf, zbuf, halo_x, halo_c, mabuf, mbuf, run_ref, h2buf, dest_v, dest_s,
         dest_sem, row_sem) = refs
    tm, d = x_ref.shape
    half = d // 2
    n_exp = rwt_ref.shape[0]
    n_chunks = tm // ROW_CHUNK
    step = pl.program_id(0) * pl.num_programs(1) + pl.program_id(1)
    n_steps = pl.num_programs(0) * pl.num_programs(1)
    slot = step % 2
    dot = functools.partial(jnp.dot, preferred_element_type=jnp.float32)

    @pl.when(pl.program_id(1) == 0)
    def _():
        halo_x[...] = jnp.zeros_like(halo_x)
        halo_c[...] = jnp.zeros_like(halo_c)

    def row_copy(which, t, k, row):
        return pltpu.make_async_copy(h2buf.at[which, pl.ds(t * PIECES, PIECES)],
                                     xs_ref.at[pl.ds(row, PIECES)], row_sem.at[which])

    def tile_rows_wait(which):
        for _ in range(TOP_K):
            pltpu.make_async_copy(h2buf.at[which], xs_ref.at[pl.ds(0, tm * PIECES)],
                                  row_sem.at[which]).wait()

    def dest_to_smem():
        dest_copy = pltpu.make_async_copy(dest_v, dest_s, dest_sem)
        dest_copy.start()
        dest_copy.wait()

    def gather_copy(which, t, k, row):
        return pltpu.make_async_copy(y_ref.at[pl.ds(row, PIECES)],
                                     gbuf.at[which, k, pl.ds(t * PIECES, PIECES)],
                                     gat_sem.at[which])

    def tile_gather_wait(which):
        for k in range(TOP_K):
            pltpu.make_async_copy(y_ref.at[pl.ds(0, tm * PIECES)], gbuf.at[which, k],
                                  gat_sem.at[which]).wait()

    @pl.when(step == 0)
    def _():
        run_ref[...] = jnp.zeros_like(run_ref)
        h2buf[1] = jnp.zeros((tm * PIECES, V7X_LANES), jnp.uint32)
        spare = (n_exp * capacity + lax.broadcasted_iota(jnp.int32, (TOP_K, tm), 0) * tm
                 + lax.broadcasted_iota(jnp.int32, (TOP_K, tm), 1))
        dest_v[...] = spare * PIECES
        dest_to_smem()
        if fused:
            wt_v[...] = jnp.zeros_like(wt_v)

            def first_gather(t, carry):
                for k in range(TOP_K):
                    gather_copy(0, t, k, pdest_cur[k * tm + t]).start()
                return carry

            lax.fori_loop(0, tm, first_gather, 0)

    per_section = tm // MATMUL_SECTIONS

    def issue_rows(section):
        for t in range(section * per_section, (section + 1) * per_section):
            for k in range(TOP_K):
                row_copy(1 - slot, t, k, dest_s[k, t]).start()
                if fused:
                    gather_copy(1 - slot, t, k, pdest_nxt[k * tm + t]).start()

    def project(section, lhs, w_ref, col):
        issue_rows(section)
        return dot(lhs, w_ref[:, col * d:(col + 1) * d])

    mod = mod_ref[...]
    sh_m, sc_m, g_m = mod[:, 0:d], mod[:, d:2 * d], mod[:, 2 * d:3 * d]
    sh_f, sc_f = mod[:, 3 * d:4 * d], mod[:, 4 * d:5 * d]

    if fused:
        tile_gather_wait(slot)
        wt_v[0:TOP_K, :] = pw_ref[...]
        w = wt_v[...].T
        acc_hi, acc_lo = None, None
        for k in range(TOP_K):
            y_hi, y_lo = _unpack_rows(_load_rows(gbuf.at[slot, k], tm))
            wk = w[:, k:k + 1]
            acc_hi = wk * y_hi if acc_hi is None else acc_hi + wk * y_hi
            acc_lo = wk * y_lo if acc_lo is None else acc_lo + wk * y_lo
        g_prev = pmod_ref[:, 5 * d:6 * d]
        xbuf[:, 0:half] = x_ref[:, 0:half] + g_prev[:, 0:half] * acc_hi
        xbuf[:, half:d] = x_ref[:, half:d] + g_prev[:, half:d] * acc_lo
        xin = xbuf
    else:
        xin = x_ref

    x = xin[...]
    gs = gmix_ref[...] * (1.0 + sc_m)
    r = lax.rsqrt(jnp.mean(x * x, axis=-1, keepdims=True) + NORM_EPS)
    hbuf[...] = (x * r * gs + sh_m).astype(jnp.bfloat16)

    h = hbuf[...]
    zbuf[:, 0:d] = project(0, h, win_ref, 0)
    zbuf[:, d:2 * d] = project(1, h, win_ref, 1)
    zbuf[:, 2 * d:3 * d] = project(2, h, win_ref, 2)
    zbuf[:, 3 * d:4 * d] = project(3, h, win_ref, 5)

    cw = conv_ref[...]
    w0, w1, w2 = cw[0:1, :], cw[1:2, :], cw[2:3, :]
    row8 = lax.broadcasted_iota(jnp.int32, (V7X_SUBLANES, d), 0)

    def branch_a(i, carry):
        r0 = pl.multiple_of(i * ROW_CHUNK, ROW_CHUNK)
        rows = pl.ds(r0, ROW_CHUNK)
        p0 = zbuf[rows, 0:d] * zbuf[rows, 2 * d:3 * d]
        rp = pl.ds(pl.multiple_of(jnp.maximum(r0 - V7X_SUBLANES, 0), V7X_SUBLANES), V7X_SUBLANES)
        prev_in = zbuf[rp, 0:d] * zbuf[rp, 2 * d:3 * d]
        prev = jnp.where(i == 0, halo_x[...] * halo_c[...], prev_in)
        r1 = pltpu.roll(p0, 1, 0)
        r2 = pltpu.roll(p0, 2, 0)
        q1 = pltpu.roll(prev, 1, 0)
        q2 = pltpu.roll(prev, 2, 0)
        p1 = jnp.concatenate([jnp.where(row8 < 1, q1, r1[0:V7X_SUBLANES]), r1[V7X_SUBLANES:]], axis=0)
        p2 = jnp.concatenate([jnp.where(row8 < 2, q2, r2[0:V7X_SUBLANES]), r2[V7X_SUBLANES:]], axis=0)
        y_a = zbuf[rows, d:2 * d] * (w0 * p2 + w1 * p1 + w2 * p0)
        mabuf[rows, :] = _sigmoid(zbuf[rows, 3 * d:4 * d]) * y_a
        return carry

    lax.fori_loop(0, n_chunks, branch_a, 0)
    last8 = pl.ds(tm - V7X_SUBLANES, V7X_SUBLANES)
    halo_x[...] = zbuf[last8, 0:d]
    halo_c[...] = zbuf[last8, 2 * d:3 * d]

    zbuf[:, 0:d] = project(4, h, win_ref, 3)
    zbuf[:, d:2 * d] = project(5, h, win_ref, 4)
    zbuf[:, 2 * d:3 * d] = project(6, h, win_ref, 6)

    tri = (lax.broadcasted_iota(jnp.int32, (CHUNK, CHUNK), 1)
           <= lax.broadcasted_iota(jnp.int32, (CHUNK, CHUNK), 0))
    wm = [jnp.where(tri, wsp_ref[g], 0.0).astype(jnp.bfloat16) for g in range(GROUPS)]
    gd = d // GROUPS
    lng = lng_ref[...]

    def branch_b(i, carry):
        r0 = pl.multiple_of(i * ROW_CHUNK, ROW_CHUNK)
        rows = pl.ds(r0, ROW_CHUNK)
        gv = _gelu(zbuf[rows, d:2 * d])
        mu = jnp.mean(gv, axis=-1, keepdims=True)
        vc = gv - mu
        var = jnp.mean(vc * vc, axis=-1, keepdims=True)
        vn = (vc * lax.rsqrt(var + NORM_EPS) * lng).astype(jnp.bfloat16)
        s = jnp.concatenate(
            [dot(wm[g], vn[:, g * gd:(g + 1) * gd]) for g in range(GROUPS)], axis=1)
        y_b = _gelu(zbuf[rows, 0:d]) * (s + bsp_ref[...])
        m = mabuf[rows, :] + _sigmoid(zbuf[rows, 2 * d:3 * d]) * y_b
        mbuf[rows, :] = m.astype(jnp.bfloat16)
        return carry

    lax.fori_loop(0, n_chunks, branch_b, 0)

    @pl.when(step > 0)
    def _():
        tile_rows_wait(slot)

    x1 = xin[...] + g_m * project(7, mbuf[...], wout_ref, 0)
    x1_ref[...] = x1
    gs2 = gffn_ref[...] * (1.0 + sc_f)
    r2n = lax.rsqrt(jnp.mean(x1 * x1, axis=-1, keepdims=True) + NORM_EPS)
    h2 = x1 * r2n * gs2 + sh_f

    nt = (((1,), (1,)), ((), ()))
    dg = functools.partial(lax.dot_general, dimension_numbers=nt,
                           preferred_element_type=jnp.float32)
    h_hi, h_lo = _split_bf16(h2)
    r_hi, r_lo = _split_bf16(rwt_ref[...])
    logits = dg(r_hi, h_hi) + dg(r_hi, h_lo) + dg(r_lo, h_hi) + rb_ref[...]

    iota_e = lax.broadcasted_iota(jnp.int32, (n_exp, tm), 0)
    vals = logits
    tops, idxs, hots = [], [], []
    for _ in range(TOP_K):
        mx = jnp.max(vals, axis=0, keepdims=True)
        ix = jnp.min(jnp.where(vals == mx, iota_e, n_exp), axis=0, keepdims=True)
        hot = iota_e == ix
        vals = jnp.where(hot, -jnp.inf, vals)
        tops.append(mx)
        idxs.append(ix)
        hots.append(hot)
    exps = [jnp.exp(t - tops[0]) for t in tops]
    denom = exps[0] + exps[1] + exps[2] + exps[3]
    for k in range(TOP_K):
        tw_ref[k:k + 1, :] = exps[k] / denom

    member = jnp.where(hots[0] | hots[1] | hots[2] | hots[3], 1.0, 0.0)
    upper = (lax.broadcasted_iota(jnp.int32, (tm, tm), 0)
             < lax.broadcasted_iota(jnp.int32, (tm, tm), 1))
    before = dot(member.astype(jnp.bfloat16), jnp.where(upper, 1.0, 0.0).astype(jnp.bfloat16))
    run = run_ref[...]
    pos = before + run[:, 0:1]
    for k in range(TOP_K):
        rk = jnp.sum(jnp.where(hots[k], pos, 0.0), axis=0, keepdims=True)
        dest_k = (idxs[k] * capacity + rk.astype(jnp.int32)) * PIECES
        dest_ref[k:k + 1, :] = dest_k
        dest_v[k:k + 1, :] = dest_k
    run = run + jnp.sum(member, axis=1, keepdims=True)
    run_ref[...] = run
    cnt_ref[...] = run.astype(jnp.int32)

    dest_copy = pltpu.make_async_copy(dest_v, dest_s, dest_sem)
    dest_copy.start()
    _store_rows(h2buf.at[slot], _pack_rows(h2))
    dest_copy.wait()

    @pl.when(step == n_steps - 1)
    def _():
        def issue(t, carry):
            for k in range(TOP_K):
                row_copy(slot, t, k, dest_s[k, t]).start()
            return carry

        lax.fori_loop(0, tm, issue, 0)
        tile_rows_wait(1 - slot)
        tile_rows_wait(slot)
        if fused:
            tile_gather_wait(1 - slot)


def _tile_major(a, tm):
    n_tok = a.shape[1]
    return a.reshape(TOP_K, n_tok // tm, tm).transpose(1, 0, 2).reshape(-1)


def _mixer(x, mod_l, gmix, win_bf, conv_w, wsp, bsp_full, lng, wout_bf, gffn, rwt, rb, prev):
    bsz, seq, d = x.shape
    n_exp = rwt.shape[0]
    tm = MIX_TILE
    n_tok = bsz * seq
    tiles = seq // tm
    n_steps = bsz * tiles
    fused = prev is not None
    const = pl.Buffered(1)

    def cspec(shape):
        return pl.BlockSpec(shape, lambda b, s: (0,) * len(shape), pipeline_mode=const)

    out_shape = (
        jax.ShapeDtypeStruct((bsz, seq, d), jnp.float32),
        jax.ShapeDtypeStruct((TOP_K, n_tok), jnp.int32),
        jax.ShapeDtypeStruct((TOP_K, n_tok), jnp.float32),
        jax.ShapeDtypeStruct((n_exp, V7X_LANES), jnp.int32),
        jax.ShapeDtypeStruct(((n_exp * n_tok + TOP_K * tm) * PIECES, V7X_LANES), jnp.uint32),
    )
    tok_t = lambda b, s: (0, b * tiles + s)
    in_specs = [
        pl.BlockSpec((None, tm, d), lambda b, s: (b, s, 0)),
        pl.BlockSpec((None, 1, 6 * d), lambda b, s: (b, 0, 0)),
        cspec((1, d)),
        cspec(win_bf.shape),
        cspec(conv_w.shape),
        cspec(wsp.shape),
        cspec(bsp_full.shape),
        cspec((1, d)),
        cspec(wout_bf.shape),
        cspec((1, d)),
        cspec(rwt.shape),
        cspec(rb.shape),
    ]
    args = [x, mod_l, gmix, win_bf, conv_w, wsp, bsp_full, lng, wout_bf, gffn, rwt, rb]
    scratch = [
        pltpu.VMEM((tm, d), jnp.bfloat16),
        pltpu.VMEM((tm, 4 * d), jnp.float32),
        pltpu.VMEM((V7X_SUBLANES, d), jnp.float32),
        pltpu.VMEM((V7X_SUBLANES, d), jnp.float32),
        pltpu.VMEM((tm, d), jnp.float32),
        pltpu.VMEM((tm, d), jnp.bfloat16),
        pltpu.VMEM((n_exp, V7X_LANES), jnp.float32),
        pltpu.VMEM((2, tm * PIECES, V7X_LANES), jnp.uint32),
        pltpu.VMEM((TOP_K, tm), jnp.int32),
        pltpu.SMEM((TOP_K, tm), jnp.int32),
        pltpu.SemaphoreType.DMA(()),
        pltpu.SemaphoreType.DMA((2,)),
    ]
    if fused:
        pdest, pw, pmod, y = prev
        pdest_flat = _tile_major(pdest, tm)
        in_specs += [
            pl.BlockSpec((TOP_K * tm,), lambda b, s: (b * tiles + s,), memory_space=pltpu.SMEM),
            pl.BlockSpec((TOP_K * tm,),
                         lambda b, s: (jnp.minimum(b * tiles + s + 1, n_steps - 1),),
                         memory_space=pltpu.SMEM),
            pl.BlockSpec((TOP_K, tm), tok_t),
            pl.BlockSpec((None, 1, 6 * d), lambda b, s: (b, 0, 0)),
            pl.BlockSpec(memory_space=pl.ANY),
        ]
        args += [pdest_flat, pdest_flat, pw, pmod, y]
        scratch += [
            pltpu.VMEM((tm, d), jnp.float32),
            pltpu.VMEM((2, TOP_K, tm * PIECES, V7X_LANES), jnp.uint32),
            pltpu.SemaphoreType.DMA((2,)),
            pltpu.VMEM((V7X_LANES, tm), jnp.float32),
        ]
    return pl.pallas_call(
        functools.partial(_mixer_kernel, capacity=n_tok, fused=fused),
        grid=(bsz, tiles),
        in_specs=in_specs,
        out_specs=(
            pl.BlockSpec((None, tm, d), lambda b, s: (b, s, 0)),
            pl.BlockSpec((TOP_K, tm), tok_t),
            pl.BlockSpec((TOP_K, tm), tok_t),
            pl.BlockSpec((n_exp, V7X_LANES), lambda b, s: (0, 0)),
            pl.BlockSpec(memory_space=pl.ANY),
        ),
        out_shape=out_shape,
        scratch_shapes=scratch,
        compiler_params=pltpu.CompilerParams(
            dimension_semantics=("arbitrary", "arbitrary"),
            vmem_limit_bytes=V7X_VMEM_LIMIT_BYTES),
        name="combine_mixer_router_dispatch" if fused else "mixer_router_dispatch",
    )(*args)


def _experts_kernel(blk_ref, exp_ref, nv_ref,
                    xs_ref, wgu_ref, bgu_ref, wdn_ref, bdn_ref, y_ref, wgu_bf, wdn_bf):
    i = pl.program_id(0)
    rows = xs_ref.shape[0] // PIECES
    half = PIECES * V7X_LANES
    d_ff = wdn_ref.shape[0]
    prev = jnp.maximum(i - 1, 0)
    new_expert = jnp.logical_or(i == 0, exp_ref[i] != exp_ref[prev])
    n_valid = nv_ref[i]

    @pl.when(new_expert)
    def _():
        wgu_bf[...] = wgu_ref[...].astype(jnp.bfloat16)
        wdn_bf[...] = wdn_ref[...].astype(jnp.bfloat16)

    @pl.when(n_valid > 0)
    def _():
        ridx = lax.broadcasted_iota(jnp.int32, (rows, half), 0)
        packed = jnp.where(ridx < n_valid, _load_rows(xs_ref, rows), jnp.uint32(0))
        x_hi, x_lo = _unpack_rows(packed)
        xs = jnp.concatenate([x_hi.astype(jnp.bfloat16), x_lo.astype(jnp.bfloat16)], axis=1)
        gu = jnp.dot(xs, wgu_bf[...], preferred_element_type=jnp.float32) + bgu_ref[...]
        gate = jnp.minimum(gu[:, :d_ff], SWIGLU_LIMIT)
        up = jnp.clip(gu[:, d_ff:], -SWIGLU_LIMIT, SWIGLU_LIMIT)
        act = gate * _sigmoid(SWIGLU_ALPHA * gate) * (up + 1.0)
        y = jnp.dot(act.astype(jnp.bfloat16), wdn_bf[...],
                    preferred_element_type=jnp.float32) + bdn_ref[...]
        _store_rows(y_ref, _pack_rows(y))


def _experts(items, xs, w_gu, b_gu, w_dn, b_dn, layer):
    n_rows = xs.shape[0] // PIECES
    half = PIECES * V7X_LANES
    d = 2 * half
    blk, exp, n_valid = items
    n_items = blk.shape[0]
    rows = EXPERT_BLOCK
    d_ff2 = w_gu.shape[-1]
    d_ff = w_dn.shape[-2]
    n_exp = w_gu.shape[1]
    depth = w_gu.shape[0]
    wmap = lambda i, blk, exp, nv: (layer, exp[i], 0, 0)
    grid_spec = pltpu.PrefetchScalarGridSpec(
        num_scalar_prefetch=3,
        grid=(n_items,),
        in_specs=[
            pl.BlockSpec((rows * PIECES, V7X_LANES), lambda i, blk, exp, nv: (blk[i], 0)),
            pl.BlockSpec((None, None, d, d_ff2), wmap),
            pl.BlockSpec((None, None, 1, d_ff2), wmap),
            pl.BlockSpec((None, None, d_ff, d), wmap),
            pl.BlockSpec((None, None, 1, d), wmap),
        ],
        out_specs=pl.BlockSpec((rows * PIECES, V7X_LANES), lambda i, blk, exp, nv: (blk[i], 0)),
        scratch_shapes=[
            pltpu.VMEM((d, d_ff2), jnp.bfloat16),
            pltpu.VMEM((d_ff, d), jnp.bfloat16),
        ],
    )
    return pl.pallas_call(
        _experts_kernel,
        grid_spec=grid_spec,
        out_shape=jax.ShapeDtypeStruct((n_rows * PIECES, V7X_LANES), jnp.uint32),
        compiler_params=pltpu.CompilerParams(
            dimension_semantics=("arbitrary",),
            vmem_limit_bytes=V7X_VMEM_LIMIT_BYTES),
        name="experts",
    )(blk, exp, n_valid, xs, w_gu, b_gu.reshape(depth, n_exp, 1, d_ff2), w_dn,
      b_dn.reshape(depth, n_exp, 1, d))


def _expert_items(counts, capacity, n_assign):
    n_exp = counts.shape[0]
    rows = EXPERT_BLOCK
    n_items = n_assign // rows + n_exp
    per_exp = (counts + rows - 1) // rows
    item_end = jnp.cumsum(per_exp)
    item_start = item_end - per_exp
    total = item_end[-1]
    idx = jnp.arange(n_items, dtype=jnp.int32)
    pick = jnp.minimum(idx, total - 1)
    exp = jnp.sum((item_end[None, :] <= pick[:, None]).astype(jnp.int32), axis=1)
    sub = pick - item_start[exp]
    blk = exp * (capacity // rows) + sub
    n_valid = jnp.where(idx < total, jnp.minimum(counts[exp] - sub * rows, rows), 0)
    return blk.astype(jnp.int32), exp.astype(jnp.int32), n_valid.astype(jnp.int32)


def _combine_kernel(dest_cur, dest_nxt, x1_ref, w_ref, mod_ref, fin_ref, y_ref, o_ref,
                    gbuf, wt_v, sem):
    tmd, d = x1_ref.shape
    half = d // 2
    step = pl.program_id(0) * pl.num_programs(1) + pl.program_id(1)
    n_steps = pl.num_programs(0) * pl.num_programs(1)
    slot = step % 2

    def gather_copy(which, t, k, row):
        return pltpu.make_async_copy(y_ref.at[pl.ds(row, PIECES)],
                                     gbuf.at[which, k, pl.ds(t * PIECES, PIECES)], sem.at[which])

    def tile_gather_wait(which):
        for k in range(TOP_K):
            pltpu.make_async_copy(y_ref.at[pl.ds(0, tmd * PIECES)], gbuf.at[which, k],
                                  sem.at[which]).wait()

    @pl.when(step == 0)
    def _():
        wt_v[...] = jnp.zeros_like(wt_v)

        def first_gather(t, carry):
            for k in range(TOP_K):
                gather_copy(0, t, k, dest_cur[k * tmd + t]).start()
            return carry

        lax.fori_loop(0, tmd, first_gather, 0)

    for t in range(tmd):
        for k in range(TOP_K):
            gather_copy(1 - slot, t, k, dest_nxt[k * tmd + t]).start()

    tile_gather_wait(slot)
    wt_v[0:TOP_K, :] = w_ref[...]
    w = wt_v[...].T
    acc_hi, acc_lo = None, None
    for k in range(TOP_K):
        y_hi, y_lo = _unpack_rows(_load_rows(gbuf.at[slot, k], tmd))
        wk = w[:, k:k + 1]
        acc_hi = wk * y_hi if acc_hi is None else acc_hi + wk * y_hi
        acc_lo = wk * y_lo if acc_lo is None else acc_lo + wk * y_lo
    g_f = mod_ref[:, 5 * d:6 * d]
    x_hi = x1_ref[:, 0:half] + g_f[:, 0:half] * acc_hi
    x_lo = x1_ref[:, half:d] + g_f[:, half:d] * acc_lo
    ms = (jnp.sum(x_hi * x_hi, axis=-1, keepdims=True)
          + jnp.sum(x_lo * x_lo, axis=-1, keepdims=True)) * (1.0 / d)
    r = lax.rsqrt(ms + NORM_EPS)
    o_ref[:, 0:half] = x_hi * r * fin_ref[:, 0:half]
    o_ref[:, half:d] = x_lo * r * fin_ref[:, half:d]

    @pl.when(step == n_steps - 1)
    def _():
        tile_gather_wait(1 - slot)


def _combine(dest, x1, top_w, mod_l, fin_g, y):
    bsz, seq, d = x1.shape
    tmd = COMBINE_TILE
    tiles = seq // tmd
    n_steps = bsz * tiles
    dest_flat = _tile_major(dest, tmd)
    return pl.pallas_call(
        _combine_kernel,
        grid=(bsz, tiles),
        in_specs=[
            pl.BlockSpec((tmd * TOP_K,), lambda b, s: (b * tiles + s,), memory_space=pltpu.SMEM),
            pl.BlockSpec((tmd * TOP_K,),
                         lambda b, s: (jnp.minimum(b * tiles + s + 1, n_steps - 1),),
                         memory_space=pltpu.SMEM),
            pl.BlockSpec((None, tmd, d), lambda b, s: (b, s, 0)),
            pl.BlockSpec((TOP_K, tmd), lambda b, s: (0, b * tiles + s)),
            pl.BlockSpec((None, 1, 6 * d), lambda b, s: (b, 0, 0)),
            pl.BlockSpec((1, d), lambda b, s: (0, 0)),
            pl.BlockSpec(memory_space=pl.ANY),
        ],
        out_specs=pl.BlockSpec((None, tmd, d), lambda b, s: (b, s, 0)),
        out_shape=jax.ShapeDtypeStruct((bsz, seq, d), jnp.float32),
        scratch_shapes=[
            pltpu.VMEM((2, TOP_K, tmd * PIECES, V7X_LANES), jnp.uint32),
            pltpu.VMEM((V7X_LANES, tmd), jnp.float32),
            pltpu.SemaphoreType.DMA((2,)),
        ],
        compiler_params=pltpu.CompilerParams(dimension_semantics=("arbitrary", "arbitrary")),
        name="final_combine",
    )(dest_flat, dest_flat, x1, top_w, mod_l, fin_g, y)


def kernel(x, c, w_ada, b_ada, norm_mix_g, w_in, conv_w, w_spatial, b_spatial, ln_v_g, w_out,
           norm_ffn_g, router_w, router_b, w_gate_up, b_gate_up, w_down, b_down, final_norm_g):
    bsz, seq, d = x.shape
    depth = w_ada.shape[0]
    n_exp = router_w.shape[-1]
    n_tok = bsz * seq
    assert seq % MIX_TILE == 0 and MIX_TILE % CHUNK == 0
    assert seq % COMBINE_TILE == 0 and n_tok % EXPERT_BLOCK == 0
    assert d // 2 == PIECES * V7X_LANES

    mod = _modulation(c, w_ada, b_ada)
    win_bf = w_in.astype(jnp.bfloat16)
    wout_bf = w_out.astype(jnp.bfloat16)
    bsp_full = jnp.repeat(jnp.swapaxes(b_spatial, 1, 2), d // GROUPS, axis=2)
    rwt = jnp.swapaxes(router_w, 1, 2)
    fin_g = final_norm_g.reshape(1, d)

    prev = None
    for l in range(depth):
        x, dest, top_w, cnt, xs = _mixer(
            x, mod[l], norm_mix_g[l].reshape(1, d), win_bf[l], conv_w[l], w_spatial[l],
            bsp_full[l], ln_v_g[l].reshape(1, d), wout_bf[l], norm_ffn_g[l].reshape(1, d),
            rwt[l], router_b[l].reshape(n_exp, 1), prev)
        items = _expert_items(cnt[:, 0], n_tok, n_tok * TOP_K)
        y = _experts(items, xs, w_gate_up, b_gate_up, w_down, b_down, l)
        prev = (dest, top_w, mod[l], y)
    return _combine(dest, x, top_w, mod[depth - 1], fin_g, y)
```

```python
import functools
import math

import jax
import jax.numpy as jnp
from jax import lax
from jax.experimental import pallas as pl
from jax.experimental.pallas import tpu as pltpu

CONV_K = 3
GROUPS = 8
CHUNK = 128
TOP_K = 4
SWIGLU_LIMIT = 7.0
SWIGLU_ALPHA = 1.702
NORM_EPS = 1e-6

V7X_SUBLANES = 8
V7X_LANES = 128
V7X_VMEM_LIMIT_BYTES = 56 * 1024 * 1024

MIX_TILE = 512
ROW_CHUNK = CHUNK
COMBINE_TILE = 256
EXPERT_BLOCK = 512
PIECES = 4
MATMUL_SECTIONS = 8


def _split_bf16(a):
    hi = a.astype(jnp.bfloat16)
    lo = (a - hi.astype(jnp.float32)).astype(jnp.bfloat16)
    return hi, lo


def _gelu(a):
    return 0.5 * a * (1.0 + lax.erf(a * (1.0 / math.sqrt(2.0))))


def _sigmoid(a):
    return 1.0 / (1.0 + jnp.exp(-a))


def _pack_rows(a):
    half = a.shape[1] // 2

    def bits(v):
        return lax.bitcast_convert_type(v.astype(jnp.bfloat16).astype(jnp.float32), jnp.uint32)

    return bits(a[:, :half]) | (bits(a[:, half:]) >> 16)


def _unpack_rows(p):
    hi = lax.bitcast_convert_type(p & jnp.uint32(0xFFFF0000), jnp.float32)
    lo = lax.bitcast_convert_type(p << 16, jnp.float32)
    return hi, lo


def _store_rows(ref, packed):
    rows = packed.shape[0]
    for c in range(PIECES):
        ref[pl.ds(c, rows, stride=PIECES), :] = packed[:, c * V7X_LANES:(c + 1) * V7X_LANES]


def _load_rows(ref, rows):
    return jnp.concatenate([ref[pl.ds(c, rows, stride=PIECES), :] for c in range(PIECES)], axis=1)


def _mod_kernel(c_ref, w_ref, b_ref, o_ref):
    c = c_ref[...]
    s = c * _sigmoid(c)
    s_hi, s_lo = _split_bf16(s)
    w_hi, w_lo = _split_bf16(w_ref[...])
    dot = functools.partial(jnp.dot, preferred_element_type=jnp.float32)
    o_ref[...] = dot(s_hi, w_hi) + dot(s_lo, w_hi) + dot(s_hi, w_lo) + b_ref[...]


def _modulation(c, w_ada, b_ada):
    depth, d, d6 = w_ada.shape
    bsz = c.shape[0]
    rows = -(-bsz // V7X_SUBLANES) * V7X_SUBLANES
    c_pad = jnp.zeros((rows, d), jnp.float32).at[:bsz].set(c)
    n_col = d6 // d
    out = pl.pallas_call(
        _mod_kernel,
        grid=(depth, n_col),
        in_specs=[
            pl.BlockSpec((rows, d), lambda l, j: (0, 0)),
            pl.BlockSpec((None, d, d), lambda l, j: (l, 0, j)),
            pl.BlockSpec((None, 1, d), lambda l, j: (l, 0, j)),
        ],
        out_specs=pl.BlockSpec((None, rows, d), lambda l, j: (l, 0, j)),
        out_shape=jax.ShapeDtypeStruct((depth, rows, d6), jnp.float32),
        name="adaln_modulation",
    )(c_pad, w_ada, b_ada.reshape(depth, 1, d6))
    return out[:, :bsz].reshape(depth, bsz, 1, d6)


def _mixer_kernel(*refs, capacity, fused):
    if fused:
        (x_ref, mod_ref, gmix_ref, win_ref, conv_ref, wsp_ref, bsp_ref, lng_ref, wout_ref,
         gffn_ref, rwt_ref, rb_ref, pdest_cur, pdest_nxt, pw_ref, pmod_ref, y_ref,
         x1_ref, dest_ref, tw_ref, cnt_ref, xs_ref,
         hbuf, zbuf, halo_x, halo_c, mabuf, mbuf, run_ref, h2buf, dest_v, dest_s,
         dest_sem, row_sem, xbuf, gbuf, gat_sem, wt_v) = refs
    else:
        (x_ref, mod_ref, gmix_ref, win_ref, conv_ref, wsp_ref, bsp_ref, lng_ref, wout_ref,
         gffn_ref, rwt_ref, rb_ref,
         x1_ref, dest_ref, tw_ref, cnt_ref, xs_ref,
         hbuf, zbuf, halo_x, halo_c, mabuf, mbuf, run_ref, h2buf, dest_v, dest_s,
         dest_sem, row_sem) = refs
    tm, d = x_ref.shape
    half = d // 2
    n_exp = rwt_ref.shape[0]
    n_chunks = tm // ROW_CHUNK
    step = pl.program_id(0) * pl.num_programs(1) + pl.program_id(1)
    n_steps = pl.num_programs(0) * pl.num_programs(1)
    slot = step % 2
    dot = functools.partial(jnp.dot, preferred_element_type=jnp.float32)

    @pl.when(pl.program_id(1) == 0)
    def _():
        halo_x[...] = jnp.zeros_like(halo_x)
        halo_c[...] = jnp.zeros_like(halo_c)

    def row_copy(which, t, k, row):
        return pltpu.make_async_copy(h2buf.at[which, pl.ds(t * PIECES, PIECES)],
                                     xs_ref.at[pl.ds(row, PIECES)], row_sem.at[which])

    def tile_rows_wait(which):
        for _ in range(TOP_K):
            pltpu.make_async_copy(h2buf.at[which], xs_ref.at[pl.ds(0, tm * PIECES)],
                                  row_sem.at[which]).wait()

    def dest_to_smem():
        dest_copy = pltpu.make_async_copy(dest_v, dest_s, dest_sem)
        dest_copy.start()
        dest_copy.wait()

    def gather_copy(which, t, k, row):
        return pltpu.make_async_copy(y_ref.at[pl.ds(row, PIECES)],
                                     gbuf.at[which, k, pl.ds(t * PIECES, PIECES)],
                                     gat_sem.at[which])

    def tile_gather_wait(which):
        for k in range(TOP_K):
            pltpu.make_async_copy(y_ref.at[pl.ds(0, tm * PIECES)], gbuf.at[which, k],
                                  gat_sem.at[which]).wait()

    @pl.when(step == 0)
    def _():
        run_ref[...] = jnp.zeros_like(run_ref)
        h2buf[1] = jnp.zeros((tm * PIECES, V7X_LANES), jnp.uint32)
        spare = (n_exp * capacity + lax.broadcasted_iota(jnp.int32, (TOP_K, tm), 0) * tm
                 + lax.broadcasted_iota(jnp.int32, (TOP_K, tm), 1))
        dest_v[...] = spare * PIECES
        dest_to_smem()
        if fused:
            wt_v[...] = jnp.zeros_like(wt_v)

            def first_gather(t, carry):
                for k in range(TOP_K):
                    gather_copy(0, t, k, pdest_cur[k * tm + t]).start()
                return carry

            lax.fori_loop(0, tm, first_gather, 0)

    per_section = tm // MATMUL_SECTIONS

    def issue_rows(section):
        for t in range(section * per_section, (section + 1) * per_section):
            for k in range(TOP_K):
                row_copy(1 - slot, t, k, dest_s[k, t]).start()
                if fused:
                    gather_copy(1 - slot, t, k, pdest_nxt[k * tm + t]).start()

    def project(section, lhs, w_ref, col):
        issue_rows(section)
        return dot(lhs, w_ref[:, col * d:(col + 1) * d])

    mod = mod_ref[...]
    sh_m, sc_m, g_m = mod[:, 0:d], mod[:, d:2 * d], mod[:, 2 * d:3 * d]
    sh_f, sc_f = mod[:, 3 * d:4 * d], mod[:, 4 * d:5 * d]

    if fused:
        tile_gather_wait(slot)
        wt_v[0:TOP_K, :] = pw_ref[...]
        w = wt_v[...].T
        acc_hi, acc_lo = None, None
        for k in range(TOP_K):
            y_hi, y_lo = _unpack_rows(_load_rows(gbuf.at[slot, k], tm))
            wk = w[:, k:k + 1]
            acc_hi = wk * y_hi if acc_hi is None else acc_hi + wk * y_hi
            acc_lo = wk * y_lo if acc_lo is None else acc_lo + wk * y_lo
        g_prev = pmod_ref[:, 5 * d:6 * d]
        xbuf[:, 0:half] = x_ref[:, 0:half] + g_prev[:, 0:half] * acc_hi
        xbuf[:, half:d] = x_ref[:, half:d] + g_prev[:, half:d] * acc_lo
        xin = xbuf
    else:
        xin = x_ref

    x = xin[...]
    gs = gmix_ref[...] * (1.0 + sc_m)
    r = lax.rsqrt(jnp.mean(x * x, axis=-1, keepdims=True) + NORM_EPS)
    hbuf[...] = (x * r * gs + sh_m).astype(jnp.bfloat16)

    h = hbuf[...]
    zbuf[:, 0:d] = project(0, h, win_ref, 0)
    zbuf[:, d:2 * d] = project(1, h, win_ref, 1)
    zbuf[:, 2 * d:3 * d] = project(2, h, win_ref, 2)
    zbuf[:, 3 * d:4 * d] = project(3, h, win_ref, 5)

    cw = conv_ref[...]
    w0, w1, w2 = cw[0:1, :], cw[1:2, :], cw[2:3, :]
    row8 = lax.broadcasted_iota(jnp.int32, (V7X_SUBLANES, d), 0)

    def branch_a(i, carry):
        r0 = pl.multiple_of(i * ROW_CHUNK, ROW_CHUNK)
        rows = pl.ds(r0, ROW_CHUNK)
        p0 = zbuf[rows, 0:d] * zbuf[rows, 2 * d:3 * d]
        rp = pl.ds(pl.multiple_of(jnp.maximum(r0 - V7X_SUBLANES, 0), V7X_SUBLANES), V7X_SUBLANES)
        prev_in = zbuf[rp, 0:d] * zbuf[rp, 2 * d:3 * d]
        prev = jnp.where(i == 0, halo_x[...] * halo_c[...], prev_in)
        r1 = pltpu.roll(p0, 1, 0)
        r2 = pltpu.roll(p0, 2, 0)
        q1 = pltpu.roll(prev, 1, 0)
        q2 = pltpu.roll(prev, 2, 0)
        p1 = jnp.concatenate([jnp.where(row8 < 1, q1, r1[0:V7X_SUBLANES]), r1[V7X_SUBLANES:]], axis=0)
        p2 = jnp.concatenate([jnp.where(row8 < 2, q2, r2[0:V7X_SUBLANES]), r2[V7X_SUBLANES:]], axis=0)
        y_a = zbuf[rows, d:2 * d] * (w0 * p2 + w1 * p1 + w2 * p0)
        mabuf[rows, :] = _sigmoid(zbuf[rows, 3 * d:4 * d]) * y_a
        return carry

    lax.fori_loop(0, n_chunks, branch_a, 0)
    last8 = pl.ds(tm - V7X_SUBLANES, V7X_SUBLANES)
    halo_x[...] = zbuf[last8, 0:d]
    halo_c[...] = zbuf[last8, 2 * d:3 * d]

    zbuf[:, 0:d] = project(4, h, win_ref, 3)
    zbuf[:, d:2 * d] = project(5, h, win_ref, 4)
    zbuf[:, 2 * d:3 * d] = project(6, h, win_ref, 6)

    tri = (lax.broadcasted_iota(jnp.int32, (CHUNK, CHUNK), 1)
           <= lax.broadcasted_iota(jnp.int32, (CHUNK, CHUNK), 0))
    wm = [jnp.where(tri, wsp_ref[g], 0.0).astype(jnp.bfloat16) for g in range(GROUPS)]
    gd = d // GROUPS
    lng = lng_ref[...]

    def branch_b(i, carry):
        r0 = pl.multiple_of(i * ROW_CHUNK, ROW_CHUNK)
        rows = pl.ds(r0, ROW_CHUNK)
        gv = _gelu(zbuf[rows, d:2 * d])
        mu = jnp.mean(gv, axis=-1, keepdims=True)
        vc = gv - mu
        var = jnp.mean(vc * vc, axis=-1, keepdims=True)
        vn = (vc * lax.rsqrt(var + NORM_EPS) * lng).astype(jnp.bfloat16)
        s = jnp.concatenate(
            [dot(wm[g], vn[:, g * gd:(g + 1) * gd]) for g in range(GROUPS)], axis=1)
        y_b = _gelu(zbuf[rows, 0:d]) * (s + bsp_ref[...])
        m = mabuf[rows, :] + _sigmoid(zbuf[rows, 2 * d:3 * d]) * y_b
        mbuf[rows, :] = m.astype(jnp.bfloat16)
        return carry

    lax.fori_loop(0, n_chunks, branch_b, 0)

    @pl.when(step > 0)
    def _():
        tile_rows_wait(slot)

    x1 = xin[...] + g_m * project(7, mbuf[...], wout_ref, 0)
    x1_ref[...] = x1
    gs2 = gffn_ref[...] * (1.0 + sc_f)
    r2n = lax.rsqrt(jnp.mean(x1 * x1, axis=-1, keepdims=True) + NORM_EPS)
    h2 = x1 * r2n * gs2 + sh_f

    nt = (((1,), (1,)), ((), ()))
    dg = functools.partial(lax.dot_general, dimension_numbers=nt,
                           preferred_element_type=jnp.float32)
    h_hi, h_lo = _split_bf16(h2)
    r_hi, r_lo = _split_bf16(rwt_ref[...])
    logits = dg(r_hi, h_hi) + dg(r_hi, h_lo) + dg(r_lo, h_hi) + rb_ref[...]

    iota_e = lax.broadcasted_iota(jnp.int32, (n_exp, tm), 0)
    vals = logits
    tops, idxs, hots = [], [], []
    for _ in range(TOP_K):
        mx = jnp.max(vals, axis=0, keepdims=True)
        ix = jnp.min(jnp.where(vals == mx, iota_e, n_exp), axis=0, keepdims=True)
        hot = iota_e == ix
        vals = jnp.where(hot, -jnp.inf, vals)
        tops.append(mx)
        idxs.append(ix)
        hots.append(hot)
    exps = [jnp.exp(t - tops[0]) for t in tops]
    denom = exps[0] + exps[1] + exps[2] + exps[3]
    for k in range(TOP_K):
        tw_ref[k:k + 1, :] = exps[k] / denom

    member = jnp.where(hots[0] | hots[1] | hots[2] | hots[3], 1.0, 0.0)
    upper = (lax.broadcasted_iota(jnp.int32, (tm, tm), 0)
             < lax.broadcasted_iota(jnp.int32, (tm, tm), 1))
    before = dot(member.astype(jnp.bfloat16), jnp.where(upper, 1.0, 0.0).astype(jnp.bfloat16))
    run = run_ref[...]
    pos = before + run[:, 0:1]
    for k in range(TOP_K):
        rk = jnp.sum(jnp.where(hots[k], pos, 0.0), axis=0, keepdims=True)
        dest_k = (idxs[k] * capacity + rk.astype(jnp.int32)) * PIECES
        dest_ref[k:k + 1, :] = dest_k
        dest_v[k:k + 1, :] = dest_k
    run = run + jnp.sum(member, axis=1, keepdims=True)
    run_ref[...] = run
    cnt_ref[...] = run.astype(jnp.int32)

    dest_copy = pltpu.make_async_copy(dest_v, dest_s, dest_sem)
    dest_copy.start()
    _store_rows(h2buf.at[slot], _pack_rows(h2))
    dest_copy.wait()

    @pl.when(step == n_steps - 1)
    def _():
        def issue(t, carry):
            for k in range(TOP_K):
                row_copy(slot, t, k, dest_s[k, t]).start()
            return carry

        lax.fori_loop(0, tm, issue, 0)
        tile_rows_wait(1 - slot)
        tile_rows_wait(slot)
        if fused:
            tile_gather_wait(1 - slot)


def _tile_major(a, tm):
    n_tok = a.shape[1]
    return a.reshape(TOP_K, n_tok // tm, tm).transpose(1, 0, 2).reshape(-1)


def _mixer(x, mod_l, gmix, win_bf, conv_w, wsp, bsp_full, lng, wout_bf, gffn, rwt, rb, prev,
           layer):
    bsz, seq, d = x.shape
    n_exp = rwt.shape[0]
    tm = MIX_TILE
    n_tok = bsz * seq
    tiles = seq // tm
    n_steps = bsz * tiles
    fused = prev is not None
    const = pl.Buffered(1)

    def cspec(shape):
        return pl.BlockSpec(shape, lambda b, s: (0,) * len(shape), pipeline_mode=const)

    def layer_spec(stack):
        return pl.BlockSpec((None,) + stack.shape[1:], lambda b, s: (layer, 0, 0),
                            pipeline_mode=const)

    out_shape = (
        jax.ShapeDtypeStruct((bsz, seq, d), jnp.float32),
        jax.ShapeDtypeStruct((TOP_K, n_tok), jnp.int32),
        jax.ShapeDtypeStruct((TOP_K, n_tok), jnp.float32),
        jax.ShapeDtypeStruct((n_exp, V7X_LANES), jnp.int32),
        jax.ShapeDtypeStruct(((n_exp * n_tok + TOP_K * tm) * PIECES, V7X_LANES), jnp.uint32),
    )
    tok_t = lambda b, s: (0, b * tiles + s)
    in_specs = [
        pl.BlockSpec((None, tm, d), lambda b, s: (b, s, 0)),
        pl.BlockSpec((None, 1, 6 * d), lambda b, s: (b, 0, 0)),
        cspec((1, d)),
        layer_spec(win_bf),
        cspec(conv_w.shape),
        cspec(wsp.shape),
        cspec(bsp_full.shape),
        cspec((1, d)),
        layer_spec(wout_bf),
        cspec((1, d)),
        cspec(rwt.shape),
        cspec(rb.shape),
    ]
    args = [x, mod_l, gmix, win_bf, conv_w, wsp, bsp_full, lng, wout_bf, gffn, rwt, rb]
    scratch = [
        pltpu.VMEM((tm, d), jnp.bfloat16),
        pltpu.VMEM((tm, 4 * d), jnp.float32),
        pltpu.VMEM((V7X_SUBLANES, d), jnp.float32),
        pltpu.VMEM((V7X_SUBLANES, d), jnp.float32),
        pltpu.VMEM((tm, d), jnp.float32),
        pltpu.VMEM((tm, d), jnp.bfloat16),
        pltpu.VMEM((n_exp, V7X_LANES), jnp.float32),
        pltpu.VMEM((2, tm * PIECES, V7X_LANES), jnp.uint32),
        pltpu.VMEM((TOP_K, tm), jnp.int32),
        pltpu.SMEM((TOP_K, tm), jnp.int32),
        pltpu.SemaphoreType.DMA(()),
        pltpu.SemaphoreType.DMA((2,)),
    ]
    if fused:
        pdest, pw, pmod, y = prev
        pdest_flat = _tile_major(pdest, tm)
        in_specs += [
            pl.BlockSpec((TOP_K * tm,), lambda b, s: (b * tiles + s,), memory_space=pltpu.SMEM),
            pl.BlockSpec((TOP_K * tm,),
                         lambda b, s: (jnp.minimum(b * tiles + s + 1, n_steps - 1),),
                         memory_space=pltpu.SMEM),
            pl.BlockSpec((TOP_K, tm), tok_t),
            pl.BlockSpec((None, 1, 6 * d), lambda b, s: (b, 0, 0)),
            pl.BlockSpec(memory_space=pl.ANY),
        ]
        args += [pdest_flat, pdest_flat, pw, pmod, y]
        scratch += [
            pltpu.VMEM((tm, d), jnp.float32),
            pltpu.VMEM((2, TOP_K, tm * PIECES, V7X_LANES), jnp.uint32),
            pltpu.SemaphoreType.DMA((2,)),
            pltpu.VMEM((V7X_LANES, tm), jnp.float32),
        ]
    return pl.pallas_call(
        functools.partial(_mixer_kernel, capacity=n_tok, fused=fused),
        grid=(bsz, tiles),
        in_specs=in_specs,
        out_specs=(
            pl.BlockSpec((None, tm, d), lambda b, s: (b, s, 0)),
            pl.BlockSpec((TOP_K, tm), tok_t),
            pl.BlockSpec((TOP_K, tm), tok_t),
            pl.BlockSpec((n_exp, V7X_LANES), lambda b, s: (0, 0)),
            pl.BlockSpec(memory_space=pl.ANY),
        ),
        out_shape=out_shape,
        scratch_shapes=scratch,
        compiler_params=pltpu.CompilerParams(
            dimension_semantics=("arbitrary", "arbitrary"),
            vmem_limit_bytes=V7X_VMEM_LIMIT_BYTES),
        name="combine_mixer_router_dispatch" if fused else "mixer_router_dispatch",
    )(*args)


def _experts_kernel(blk_ref, exp_ref, nv_ref,
                    xs_ref, wgu_ref, bgu_ref, wdn_ref, bdn_ref, y_ref, wgu_bf, wdn_bf):
    i = pl.program_id(0)
    rows = xs_ref.shape[0] // PIECES
    half = PIECES * V7X_LANES
    d_ff = wdn_ref.shape[0]
    prev = jnp.maximum(i - 1, 0)
    new_expert = jnp.logical_or(i == 0, exp_ref[i] != exp_ref[prev])
    n_valid = nv_ref[i]

    @pl.when(new_expert)
    def _():
        wgu_bf[...] = wgu_ref[...].astype(jnp.bfloat16)
        wdn_bf[...] = wdn_ref[...].astype(jnp.bfloat16)

    @pl.when(n_valid > 0)
    def _():
        ridx = lax.broadcasted_iota(jnp.int32, (rows, half), 0)
        packed = jnp.where(ridx < n_valid, _load_rows(xs_ref, rows), jnp.uint32(0))
        x_hi, x_lo = _unpack_rows(packed)
        xs = jnp.concatenate([x_hi.astype(jnp.bfloat16), x_lo.astype(jnp.bfloat16)], axis=1)
        gu = jnp.dot(xs, wgu_bf[...], preferred_element_type=jnp.float32) + bgu_ref[...]
        gate = jnp.minimum(gu[:, :d_ff], SWIGLU_LIMIT)
        up = jnp.clip(gu[:, d_ff:], -SWIGLU_LIMIT, SWIGLU_LIMIT)
        act = gate * _sigmoid(SWIGLU_ALPHA * gate) * (up + 1.0)
        y = jnp.dot(act.astype(jnp.bfloat16), wdn_bf[...],
                    preferred_element_type=jnp.float32) + bdn_ref[...]
        _store_rows(y_ref, _pack_rows(y))


def _experts(items, xs, w_gu, b_gu, w_dn, b_dn, layer):
    n_rows = xs.shape[0] // PIECES
    half = PIECES * V7X_LANES
    d = 2 * half
    blk, exp, n_valid = items
    n_items = blk.shape[0]
    rows = EXPERT_BLOCK
    d_ff2 = w_gu.shape[-1]
    d_ff = w_dn.shape[-2]
    n_exp = w_gu.shape[1]
    depth = w_gu.shape[0]
    wmap = lambda i, blk, exp, nv: (layer, exp[i], 0, 0)
    grid_spec = pltpu.PrefetchScalarGridSpec(
        num_scalar_prefetch=3,
        grid=(n_items,),
        in_specs=[
            pl.BlockSpec((rows * PIECES, V7X_LANES), lambda i, blk, exp, nv: (blk[i], 0)),
            pl.BlockSpec((None, None, d, d_ff2), wmap),
            pl.BlockSpec((None, None, 1, d_ff2), wmap),
            pl.BlockSpec((None, None, d_ff, d), wmap),
            pl.BlockSpec((None, None, 1, d), wmap),
        ],
        out_specs=pl.BlockSpec((rows * PIECES, V7X_LANES), lambda i, blk, exp, nv: (blk[i], 0)),
        scratch_shapes=[
            pltpu.VMEM((d, d_ff2), jnp.bfloat16),
            pltpu.VMEM((d_ff, d), jnp.bfloat16),
        ],
    )
    return pl.pallas_call(
        _experts_kernel,
        grid_spec=grid_spec,
        out_shape=jax.ShapeDtypeStruct((n_rows * PIECES, V7X_LANES), jnp.uint32),
        compiler_params=pltpu.CompilerParams(
            dimension_semantics=("arbitrary",),
            vmem_limit_bytes=V7X_VMEM_LIMIT_BYTES),
        name="experts",
    )(blk, exp, n_valid, xs, w_gu, b_gu.reshape(depth, n_exp, 1, d_ff2), w_dn,
      b_dn.reshape(depth, n_exp, 1, d))


def _expert_items(counts, capacity, n_assign):
    n_exp = counts.shape[0]
    rows = EXPERT_BLOCK
    n_items = n_assign // rows + n_exp
    per_exp = (counts + rows - 1) // rows
    item_end = jnp.cumsum(per_exp)
    item_start = item_end - per_exp
    total = item_end[-1]
    idx = jnp.arange(n_items, dtype=jnp.int32)
    pick = jnp.minimum(idx, total - 1)
    exp = jnp.sum((item_end[None, :] <= pick[:, None]).astype(jnp.int32), axis=1)
    hot = exp[:, None] == jnp.arange(n_exp, dtype=jnp.int32)[None, :]
    start_e = jnp.sum(jnp.where(hot, item_start[None, :], 0), axis=1)
    count_e = jnp.sum(jnp.where(hot, counts[None, :], 0), axis=1)
    sub = pick - start_e
    blk = exp * (capacity // rows) + sub
    n_valid = jnp.where(idx < total, jnp.minimum(count_e - sub * rows, rows), 0)
    return blk.astype(jnp.int32), exp.astype(jnp.int32), n_valid.astype(jnp.int32)


def _combine_kernel(dest_cur, dest_nxt, x1_ref, w_ref, mod_ref, fin_ref, y_ref, o_ref,
                    gbuf, wt_v, sem):
    tmd, d = x1_ref.shape
    half = d // 2
    step = pl.program_id(0) * pl.num_programs(1) + pl.program_id(1)
    n_steps = pl.num_programs(0) * pl.num_programs(1)
    slot = step % 2

    def gather_copy(which, t, k, row):
        return pltpu.make_async_copy(y_ref.at[pl.ds(row, PIECES)],
                                     gbuf.at[which, k, pl.ds(t * PIECES, PIECES)], sem.at[which])

    def tile_gather_wait(which):
        for k in range(TOP_K):
            pltpu.make_async_copy(y_ref.at[pl.ds(0, tmd * PIECES)], gbuf.at[which, k],
                                  sem.at[which]).wait()

    @pl.when(step == 0)
    def _():
        wt_v[...] = jnp.zeros_like(wt_v)

        def first_gather(t, carry):
            for k in range(TOP_K):
                gather_copy(0, t, k, dest_cur[k * tmd + t]).start()
            return carry

        lax.fori_loop(0, tmd, first_gather, 0)

    for t in range(tmd):
        for k in range(TOP_K):
            gather_copy(1 - slot, t, k, dest_nxt[k * tmd + t]).start()

    tile_gather_wait(slot)
    wt_v[0:TOP_K, :] = w_ref[...]
    w = wt_v[...].T
    acc_hi, acc_lo = None, None
    for k in range(TOP_K):
        y_hi, y_lo = _unpack_rows(_load_rows(gbuf.at[slot, k], tmd))
        wk = w[:, k:k + 1]
        acc_hi = wk * y_hi if acc_hi is None else acc_hi + wk * y_hi
        acc_lo = wk * y_lo if acc_lo is None else acc_lo + wk * y_lo
    g_f = mod_ref[:, 5 * d:6 * d]
    x_hi = x1_ref[:, 0:half] + g_f[:, 0:half] * acc_hi
    x_lo = x1_ref[:, half:d] + g_f[:, half:d] * acc_lo
    ms = (jnp.sum(x_hi * x_hi, axis=-1, keepdims=True)
          + jnp.sum(x_lo * x_lo, axis=-1, keepdims=True)) * (1.0 / d)
    r = lax.rsqrt(ms + NORM_EPS)
    o_ref[:, 0:half] = x_hi * r * fin_ref[:, 0:half]
    o_ref[:, half:d] = x_lo * r * fin_ref[:, half:d]

    @pl.when(step == n_steps - 1)
    def _():
        tile_gather_wait(1 - slot)


def _combine(dest, x1, top_w, mod_l, fin_g, y):
    bsz, seq, d = x1.shape
    tmd = COMBINE_TILE
    tiles = seq // tmd
    n_steps = bsz * tiles
    dest_flat = _tile_major(dest, tmd)
    return pl.pallas_call(
        _combine_kernel,
        grid=(bsz, tiles),
        in_specs=[
            pl.BlockSpec((tmd * TOP_K,), lambda b, s: (b * tiles + s,), memory_space=pltpu.SMEM),
            pl.BlockSpec((tmd * TOP_K,),
                         lambda b, s: (jnp.minimum(b * tiles + s + 1, n_steps - 1),),
                         memory_space=pltpu.SMEM),
            pl.BlockSpec((None, tmd, d), lambda b, s: (b, s, 0)),
            pl.BlockSpec((TOP_K, tmd), lambda b, s: (0, b * tiles + s)),
            pl.BlockSpec((None, 1, 6 * d), lambda b, s: (b, 0, 0)),
            pl.BlockSpec((1, d), lambda b, s: (0, 0)),
            pl.BlockSpec(memory_space=pl.ANY),
        ],
        out_specs=pl.BlockSpec((None, tmd, d), lambda b, s: (b, s, 0)),
        out_shape=jax.ShapeDtypeStruct((bsz, seq, d), jnp.float32),
        scratch_shapes=[
            pltpu.VMEM((2, TOP_K, tmd * PIECES, V7X_LANES), jnp.uint32),
            pltpu.VMEM((V7X_LANES, tmd), jnp.float32),
            pltpu.SemaphoreType.DMA((2,)),
        ],
        compiler_params=pltpu.CompilerParams(dimension_semantics=("arbitrary", "arbitrary")),
        name="final_combine",
    )(dest_flat, dest_flat, x1, top_w, mod_l, fin_g, y)


def kernel(x, c, w_ada, b_ada, norm_mix_g, w_in, conv_w, w_spatial, b_spatial, ln_v_g, w_out,
           norm_ffn_g, router_w, router_b, w_gate_up, b_gate_up, w_down, b_down, final_norm_g):
    bsz, seq, d = x.shape
    depth = w_ada.shape[0]
    n_exp = router_w.shape[-1]
    n_tok = bsz * seq
    assert seq % MIX_TILE == 0 and MIX_TILE % CHUNK == 0
    assert seq % COMBINE_TILE == 0 and n_tok % EXPERT_BLOCK == 0
    assert d // 2 == PIECES * V7X_LANES

    mod = _modulation(c, w_ada, b_ada)
    win_bf = w_in.astype(jnp.bfloat16)
    wout_bf = w_out.astype(jnp.bfloat16)
    bsp_full = jnp.repeat(jnp.swapaxes(b_spatial, 1, 2), d // GROUPS, axis=2)
    rwt = jnp.swapaxes(router_w, 1, 2)
    fin_g = final_norm_g.reshape(1, d)

    prev = None
    for l in range(depth):
        x, dest, top_w, cnt, xs = _mixer(
            x, mod[l], norm_mix_g[l].reshape(1, d), win_bf, conv_w[l], w_spatial[l],
            bsp_full[l], ln_v_g[l].reshape(1, d), wout_bf, norm_ffn_g[l].reshape(1, d),
            rwt[l], router_b[l].reshape(n_exp, 1), prev, l)
        items = _expert_items(cnt[:, 0], n_tok, n_tok * TOP_K)
        y = _experts(items, xs, w_gate_up, b_gate_up, w_down, b_down, l)
        prev = (dest, top_w, mod[l], y)
    return _combine(dest, x, top_w, mod[depth - 1], fin_g, y)
```

```python
import functools
import math

import jax
import jax.numpy as jnp
from jax import lax
from jax.experimental import pallas as pl
from jax.experimental.pallas import tpu as pltpu

GROUPS = 8
CHUNK = 128
TOP_K = 4
SWIGLU_LIMIT = 7.0
SWIGLU_ALPHA = 1.702
NORM_EPS = 1e-6

V7X_SUBLANES = 8
V7X_LANES = 128
V7X_VMEM_BYTES = 64 * 1024 * 1024
V7X_VMEM_LIMIT_BYTES = V7X_VMEM_BYTES * 7 // 8

MIX_TILE = 512
ROW_CHUNK = CHUNK
COMBINE_TILE = 256
EXPERT_BLOCK = 512
PIECES = 4
MATMUL_SECTIONS = 8


def _split_bf16(a):
    hi = a.astype(jnp.bfloat16)
    lo = (a - hi.astype(jnp.float32)).astype(jnp.bfloat16)
    return hi, lo


def _gelu(a):
    return 0.5 * a * (1.0 + lax.erf(a * (1.0 / math.sqrt(2.0))))


def _sigmoid(a):
    return 1.0 / (1.0 + jnp.exp(-a))


def _pack_rows(a):
    half = a.shape[1] // 2

    def bits(v):
        return lax.bitcast_convert_type(v.astype(jnp.bfloat16).astype(jnp.float32), jnp.uint32)

    return bits(a[:, :half]) | (bits(a[:, half:]) >> 16)


def _unpack_rows(p):
    hi = lax.bitcast_convert_type(p & jnp.uint32(0xFFFF0000), jnp.float32)
    lo = lax.bitcast_convert_type(p << 16, jnp.float32)
    return hi, lo


def _store_rows(ref, packed):
    rows = packed.shape[0]
    for c in range(PIECES):
        ref[pl.ds(c, rows, stride=PIECES), :] = packed[:, c * V7X_LANES:(c + 1) * V7X_LANES]


def _load_rows(ref, rows):
    return jnp.concatenate([ref[pl.ds(c, rows, stride=PIECES), :] for c in range(PIECES)], axis=1)


def _mod_kernel(c_ref, w_ref, b_ref, o_ref):
    c = c_ref[...]
    s = c * _sigmoid(c)
    s_hi, s_lo = _split_bf16(s)
    w_hi, w_lo = _split_bf16(w_ref[...])
    dot = functools.partial(jnp.dot, preferred_element_type=jnp.float32)
    o_ref[...] = dot(s_hi, w_hi) + dot(s_lo, w_hi) + dot(s_hi, w_lo) + b_ref[...]


def _modulation(c, w_ada, b_ada):
    depth, d, d6 = w_ada.shape
    bsz = c.shape[0]
    rows = -(-bsz // V7X_SUBLANES) * V7X_SUBLANES
    c_pad = jnp.zeros((rows, d), jnp.float32).at[:bsz].set(c)
    n_col = d6 // d
    out = pl.pallas_call(
        _mod_kernel,
        grid=(depth, n_col),
        in_specs=[
            pl.BlockSpec((rows, d), lambda l, j: (0, 0)),
            pl.BlockSpec((None, d, d), lambda l, j: (l, 0, j)),
            pl.BlockSpec((None, 1, d), lambda l, j: (l, 0, j)),
        ],
        out_specs=pl.BlockSpec((None, rows, d), lambda l, j: (l, 0, j)),
        out_shape=jax.ShapeDtypeStruct((depth, rows, d6), jnp.float32),
        name="adaln_modulation",
    )(c_pad, w_ada, b_ada.reshape(depth, 1, d6))
    return out[:, :bsz].reshape(depth, bsz, 1, d6)


def _mixer_kernel(*refs, capacity, fused):
    if fused:
        (x_ref, mod_ref, gmix_ref, win_ref, conv_ref, wsp_ref, bsp_ref, lng_ref, wout_ref,
         gffn_ref, rwt_ref, rb_ref, pdest_cur, pdest_nxt, pw_ref, pmod_ref, y_ref,
         x1_ref, dest_ref, tw_ref, cnt_ref, xs_ref,
         hbuf, zbuf, halo_x, halo_c, mabuf, mbuf, run_ref, h2buf, dest_v, dest_s,
         dest_sem, row_sem, xbuf, gbuf, gat_sem, wt_v) = refs
    else:
        (x_ref, mod_ref, gmix_ref, win_ref, conv_ref, wsp_ref, bsp_ref, lng_ref, wout_ref,
         gffn_ref, rwt_ref, rb_ref,
         x1_ref, dest_ref, tw_ref, cnt_ref, xs_ref,
         hbuf, zbuf, halo_x, halo_c, mabuf, mbuf, run_ref, h2buf, dest_v, dest_s,
         dest_sem, row_sem) = refs
    tm, d = x_ref.shape
    half = d // 2
    n_exp = rwt_ref.shape[0]
    n_chunks = tm // ROW_CHUNK
    step = pl.program_id(0) * pl.num_programs(1) + pl.program_id(1)
    n_steps = pl.num_programs(0) * pl.num_programs(1)
    slot = step % 2
    dot = functools.partial(jnp.dot, preferred_element_type=jnp.float32)

    @pl.when(pl.program_id(1) == 0)
    def _():
        halo_x[...] = jnp.zeros_like(halo_x)
        halo_c[...] = jnp.zeros_like(halo_c)

    def row_copy(which, t, k, row):
        return pltpu.make_async_copy(h2buf.at[which, pl.ds(t * PIECES, PIECES)],
                                     xs_ref.at[pl.ds(row, PIECES)], row_sem.at[which])

    def tile_rows_wait(which):
        for _ in range(TOP_K):
            pltpu.make_async_copy(h2buf.at[which], xs_ref.at[pl.ds(0, tm * PIECES)],
                                  row_sem.at[which]).wait()

    def dest_to_smem():
        dest_copy = pltpu.make_async_copy(dest_v, dest_s, dest_sem)
        dest_copy.start()
        dest_copy.wait()

    def gather_copy(which, t, k, row):
        return pltpu.make_async_copy(y_ref.at[pl.ds(row, PIECES)],
                                     gbuf.at[which, k, pl.ds(t * PIECES, PIECES)],
                                     gat_sem.at[which])

    def tile_gather_wait(which):
        for k in range(TOP_K):
            pltpu.make_async_copy(y_ref.at[pl.ds(0, tm * PIECES)], gbuf.at[which, k],
                                  gat_sem.at[which]).wait()

    @pl.when(step == 0)
    def _():
        run_ref[...] = jnp.zeros_like(run_ref)
        h2buf[1] = jnp.zeros((tm * PIECES, V7X_LANES), jnp.uint32)
        spare = (n_exp * capacity + lax.broadcasted_iota(jnp.int32, (TOP_K, tm), 0) * tm
                 + lax.broadcasted_iota(jnp.int32, (TOP_K, tm), 1))
        dest_v[...] = spare * PIECES
        dest_to_smem()
        if fused:
            wt_v[...] = jnp.zeros_like(wt_v)

            def first_gather(t, carry):
                for k in range(TOP_K):
                    gather_copy(0, t, k, pdest_cur[k * tm + t]).start(priority=k % 2)
                return carry

            lax.fori_loop(0, tm, first_gather, 0)

    def spread(first, last):
        n = last - first + 1
        return [0] * (first + 1) + [(tm * (s + 1)) // n for s in range(n)] + [tm] * (
            MATMUL_SECTIONS - 1 - last)

    gather_rows = spread(0, MATMUL_SECTIONS - 3)
    dispatch_rows = spread(2, MATMUL_SECTIONS - 1) if fused else spread(0, MATMUL_SECTIONS - 1)

    def issue_rows(section):
        if fused:
            for t in range(gather_rows[section], gather_rows[section + 1]):
                for k in range(TOP_K):
                    gather_copy(1 - slot, t, k, pdest_nxt[k * tm + t]).start(priority=k % 2)
        for t in range(dispatch_rows[section], dispatch_rows[section + 1]):
            for k in range(TOP_K):
                row_copy(1 - slot, t, k, dest_s[k, t]).start(priority=k % 2)

    def project(section, lhs, w_ref, col):
        issue_rows(section)
        return dot(lhs, w_ref[:, col * d:(col + 1) * d])

    mod = mod_ref[...]
    sh_m, sc_m, g_m = mod[:, 0:d], mod[:, d:2 * d], mod[:, 2 * d:3 * d]
    sh_f, sc_f = mod[:, 3 * d:4 * d], mod[:, 4 * d:5 * d]

    if fused:
        tile_gather_wait(slot)
        wt_v[0:TOP_K, :] = pw_ref[...]
        w = wt_v[...].T
        acc_hi, acc_lo = None, None
        for k in range(TOP_K):
            y_hi, y_lo = _unpack_rows(_load_rows(gbuf.at[slot, k], tm))
            wk = w[:, k:k + 1]
            acc_hi = wk * y_hi if acc_hi is None else acc_hi + wk * y_hi
            acc_lo = wk * y_lo if acc_lo is None else acc_lo + wk * y_lo
        g_prev = pmod_ref[:, 5 * d:6 * d]
        xbuf[:, 0:half] = x_ref[:, 0:half] + g_prev[:, 0:half] * acc_hi
        xbuf[:, half:d] = x_ref[:, half:d] + g_prev[:, half:d] * acc_lo
        xin = xbuf
    else:
        xin = x_ref

    x = xin[...]
    gs = gmix_ref[...] * (1.0 + sc_m)
    r = lax.rsqrt(jnp.mean(x * x, axis=-1, keepdims=True) + NORM_EPS)
    hbuf[...] = (x * r * gs + sh_m).astype(jnp.bfloat16)

    h = hbuf[...]
    zbuf[:, 0:d] = project(0, h, win_ref, 0)
    zbuf[:, d:2 * d] = project(1, h, win_ref, 1)
    zbuf[:, 2 * d:3 * d] = project(2, h, win_ref, 2)
    zbuf[:, 3 * d:4 * d] = project(3, h, win_ref, 5)

    cw = conv_ref[...]
    w0, w1, w2 = cw[0:1, :], cw[1:2, :], cw[2:3, :]
    row8 = lax.broadcasted_iota(jnp.int32, (V7X_SUBLANES, d), 0)

    def branch_a(i, carry):
        r0 = pl.multiple_of(i * ROW_CHUNK, ROW_CHUNK)
        rows = pl.ds(r0, ROW_CHUNK)
        p0 = zbuf[rows, 0:d] * zbuf[rows, 2 * d:3 * d]
        rp = pl.ds(pl.multiple_of(jnp.maximum(r0 - V7X_SUBLANES, 0), V7X_SUBLANES), V7X_SUBLANES)
        prev_in = zbuf[rp, 0:d] * zbuf[rp, 2 * d:3 * d]
        prev = jnp.where(i == 0, halo_x[...] * halo_c[...], prev_in)
        r1 = pltpu.roll(p0, 1, 0)
        r2 = pltpu.roll(p0, 2, 0)
        q1 = pltpu.roll(prev, 1, 0)
        q2 = pltpu.roll(prev, 2, 0)
        p1 = jnp.concatenate([jnp.where(row8 < 1, q1, r1[0:V7X_SUBLANES]), r1[V7X_SUBLANES:]], axis=0)
        p2 = jnp.concatenate([jnp.where(row8 < 2, q2, r2[0:V7X_SUBLANES]), r2[V7X_SUBLANES:]], axis=0)
        y_a = zbuf[rows, d:2 * d] * (w0 * p2 + w1 * p1 + w2 * p0)
        mabuf[rows, :] = _sigmoid(zbuf[rows, 3 * d:4 * d]) * y_a
        return carry

    lax.fori_loop(0, n_chunks, branch_a, 0)
    last8 = pl.ds(tm - V7X_SUBLANES, V7X_SUBLANES)
    halo_x[...] = zbuf[last8, 0:d]
    halo_c[...] = zbuf[last8, 2 * d:3 * d]

    zbuf[:, 0:d] = project(4, h, win_ref, 3)
    zbuf[:, d:2 * d] = project(5, h, win_ref, 4)
    zbuf[:, 2 * d:3 * d] = project(6, h, win_ref, 6)

    tri = (lax.broadcasted_iota(jnp.int32, (CHUNK, CHUNK), 1)
           <= lax.broadcasted_iota(jnp.int32, (CHUNK, CHUNK), 0))
    wm = [jnp.where(tri, wsp_ref[g], 0.0).astype(jnp.bfloat16) for g in range(GROUPS)]
    gd = d // GROUPS
    lng = lng_ref[...]

    def branch_b(i, carry):
        r0 = pl.multiple_of(i * ROW_CHUNK, ROW_CHUNK)
        rows = pl.ds(r0, ROW_CHUNK)
        gv = _gelu(zbuf[rows, d:2 * d])
        mu = jnp.mean(gv, axis=-1, keepdims=True)
        vc = gv - mu
        var = jnp.mean(vc * vc, axis=-1, keepdims=True)
        vn = (vc * lax.rsqrt(var + NORM_EPS) * lng).astype(jnp.bfloat16)
        s = jnp.concatenate(
            [dot(wm[g], vn[:, g * gd:(g + 1) * gd]) for g in range(GROUPS)], axis=1)
        y_b = _gelu(zbuf[rows, 0:d]) * (s + bsp_ref[...])
        m = mabuf[rows, :] + _sigmoid(zbuf[rows, 2 * d:3 * d]) * y_b
        mbuf[rows, :] = m.astype(jnp.bfloat16)
        return carry

    lax.fori_loop(0, n_chunks, branch_b, 0)

    @pl.when(step > 0)
    def _():
        tile_rows_wait(slot)

    x1 = xin[...] + g_m * project(7, mbuf[...], wout_ref, 0)
    x1_ref[...] = x1
    gs2 = gffn_ref[...] * (1.0 + sc_f)
    r2n = lax.rsqrt(jnp.mean(x1 * x1, axis=-1, keepdims=True) + NORM_EPS)
    h2 = x1 * r2n * gs2 + sh_f

    nt = (((1,), (1,)), ((), ()))
    dg = functools.partial(lax.dot_general, dimension_numbers=nt,
                           preferred_element_type=jnp.float32)
    h_hi, h_lo = _split_bf16(h2)
    r_hi, r_lo = _split_bf16(rwt_ref[...])
    logits = dg(r_hi, h_hi) + dg(r_hi, h_lo) + dg(r_lo, h_hi) + rb_ref[...]

    iota_e = lax.broadcasted_iota(jnp.int32, (n_exp, tm), 0)
    vals = logits
    tops, idxs, hots = [], [], []
    for _ in range(TOP_K):
        mx = jnp.max(vals, axis=0, keepdims=True)
        ix = jnp.min(jnp.where(vals == mx, iota_e, n_exp), axis=0, keepdims=True)
        hot = iota_e == ix
        vals = jnp.where(hot, -jnp.inf, vals)
        tops.append(mx)
        idxs.append(ix)
        hots.append(hot)
    exps = [jnp.exp(t - tops[0]) for t in tops]
    denom = exps[0] + exps[1] + exps[2] + exps[3]
    for k in range(TOP_K):
        tw_ref[k:k + 1, :] = exps[k] / denom

    member = jnp.where(hots[0] | hots[1] | hots[2] | hots[3], 1.0, 0.0)
    upper = (lax.broadcasted_iota(jnp.int32, (tm, tm), 0)
             < lax.broadcasted_iota(jnp.int32, (tm, tm), 1))
    before = dot(member.astype(jnp.bfloat16), jnp.where(upper, 1.0, 0.0).astype(jnp.bfloat16))
    run = run_ref[...]
    pos = before + run[:, 0:1]
    for k in range(TOP_K):
        rk = jnp.sum(jnp.where(hots[k], pos, 0.0), axis=0, keepdims=True)
        dest_k = (idxs[k] * capacity + rk.astype(jnp.int32)) * PIECES
        dest_ref[k:k + 1, :] = dest_k
        dest_v[k:k + 1, :] = dest_k
    run = run + jnp.sum(member, axis=1, keepdims=True)
    run_ref[...] = run
    cnt_ref[...] = run.astype(jnp.int32)

    dest_copy = pltpu.make_async_copy(dest_v, dest_s, dest_sem)
    dest_copy.start()
    _store_rows(h2buf.at[slot], _pack_rows(h2))
    dest_copy.wait()

    @pl.when(step == n_steps - 1)
    def _():
        def issue(t, carry):
            for k in range(TOP_K):
                row_copy(slot, t, k, dest_s[k, t]).start(priority=k % 2)
            return carry

        lax.fori_loop(0, tm, issue, 0)
        tile_rows_wait(1 - slot)
        tile_rows_wait(slot)
        if fused:
            tile_gather_wait(1 - slot)


def _tile_major(a, tm):
    n_tok = a.shape[1]
    return a.reshape(TOP_K, n_tok // tm, tm).transpose(1, 0, 2).reshape(-1)


def _mixer(x, mod_l, gmix, win_bf, conv_w, wsp, bsp_full, lng, wout_bf, gffn, rwt, rb, prev,
           layer):
    bsz, seq, d = x.shape
    n_exp = rwt.shape[0]
    tm = MIX_TILE
    n_tok = bsz * seq
    tiles = seq // tm
    n_steps = bsz * tiles
    fused = prev is not None
    const = pl.Buffered(1)

    def cspec(shape):
        return pl.BlockSpec(shape, lambda b, s: (0,) * len(shape), pipeline_mode=const)

    def layer_spec(stack):
        return pl.BlockSpec((None,) + stack.shape[1:], lambda b, s: (layer, 0, 0),
                            pipeline_mode=const)

    out_shape = (
        jax.ShapeDtypeStruct((bsz, seq, d), jnp.float32),
        jax.ShapeDtypeStruct((TOP_K, n_tok), jnp.int32),
        jax.ShapeDtypeStruct((TOP_K, n_tok), jnp.float32),
        jax.ShapeDtypeStruct((n_exp, V7X_LANES), jnp.int32),
        jax.ShapeDtypeStruct(((n_exp * n_tok + TOP_K * tm) * PIECES, V7X_LANES), jnp.uint32),
    )
    tok_t = lambda b, s: (0, b * tiles + s)
    in_specs = [
        pl.BlockSpec((None, tm, d), lambda b, s: (b, s, 0)),
        pl.BlockSpec((None, 1, 6 * d), lambda b, s: (b, 0, 0)),
        cspec((1, d)),
        layer_spec(win_bf),
        cspec(conv_w.shape),
        cspec(wsp.shape),
        cspec(bsp_full.shape),
        cspec((1, d)),
        layer_spec(wout_bf),
        cspec((1, d)),
        cspec(rwt.shape),
        cspec(rb.shape),
    ]
    args = [x, mod_l, gmix, win_bf, conv_w, wsp, bsp_full, lng, wout_bf, gffn, rwt, rb]
    scratch = [
        pltpu.VMEM((tm, d), jnp.bfloat16),
        pltpu.VMEM((tm, 4 * d), jnp.float32),
        pltpu.VMEM((V7X_SUBLANES, d), jnp.float32),
        pltpu.VMEM((V7X_SUBLANES, d), jnp.float32),
        pltpu.VMEM((tm, d), jnp.float32),
        pltpu.VMEM((tm, d), jnp.bfloat16),
        pltpu.VMEM((n_exp, V7X_LANES), jnp.float32),
        pltpu.VMEM((2, tm * PIECES, V7X_LANES), jnp.uint32),
        pltpu.VMEM((TOP_K, tm), jnp.int32),
        pltpu.SMEM((TOP_K, tm), jnp.int32),
        pltpu.SemaphoreType.DMA(()),
        pltpu.SemaphoreType.DMA((2,)),
    ]
    if fused:
        pdest, pw, pmod, y = prev
        pdest_flat = _tile_major(pdest, tm)
        in_specs += [
            pl.BlockSpec((TOP_K * tm,), lambda b, s: (b * tiles + s,), memory_space=pltpu.SMEM),
            pl.BlockSpec((TOP_K * tm,),
                         lambda b, s: (jnp.minimum(b * tiles + s + 1, n_steps - 1),),
                         memory_space=pltpu.SMEM),
            pl.BlockSpec((TOP_K, tm), tok_t),
            pl.BlockSpec((None, 1, 6 * d), lambda b, s: (b, 0, 0)),
            pl.BlockSpec(memory_space=pl.ANY),
        ]
        args += [pdest_flat, pdest_flat, pw, pmod, y]
        scratch += [
            pltpu.VMEM((tm, d), jnp.float32),
            pltpu.VMEM((2, TOP_K, tm * PIECES, V7X_LANES), jnp.uint32),
            pltpu.SemaphoreType.DMA((2,)),
            pltpu.VMEM((V7X_LANES, tm), jnp.float32),
        ]
    return pl.pallas_call(
        functools.partial(_mixer_kernel, capacity=n_tok, fused=fused),
        grid=(bsz, tiles),
        in_specs=in_specs,
        out_specs=(
            pl.BlockSpec((None, tm, d), lambda b, s: (b, s, 0)),
            pl.BlockSpec((TOP_K, tm), tok_t),
            pl.BlockSpec((TOP_K, tm), tok_t),
            pl.BlockSpec((n_exp, V7X_LANES), lambda b, s: (0, 0)),
            pl.BlockSpec(memory_space=pl.ANY),
        ),
        out_shape=out_shape,
        scratch_shapes=scratch,
        compiler_params=pltpu.CompilerParams(
            dimension_semantics=("arbitrary", "arbitrary"),
            vmem_limit_bytes=V7X_VMEM_LIMIT_BYTES),
        name="combine_mixer_router_dispatch" if fused else "mixer_router_dispatch",
    )(*args)


def _experts_kernel(blk_ref, exp_ref, nv_ref,
                    xs_ref, wgu_ref, bgu_ref, wdn_ref, bdn_ref, y_ref, wgu_bf, wdn_bf):
    i = pl.program_id(0)
    rows = xs_ref.shape[0] // PIECES
    half = PIECES * V7X_LANES
    d_ff = wdn_ref.shape[0]
    prev = jnp.maximum(i - 1, 0)
    new_expert = jnp.logical_or(i == 0, exp_ref[i] != exp_ref[prev])
    n_valid = nv_ref[i]

    @pl.when(new_expert)
    def _():
        wgu_bf[...] = wgu_ref[...].astype(jnp.bfloat16)
        wdn_bf[...] = wdn_ref[...].astype(jnp.bfloat16)

    @pl.when(n_valid > 0)
    def _():
        ridx = lax.broadcasted_iota(jnp.int32, (rows, half), 0)
        packed = jnp.where(ridx < n_valid, _load_rows(xs_ref, rows), jnp.uint32(0))
        x_hi, x_lo = _unpack_rows(packed)
        xs = jnp.concatenate([x_hi.astype(jnp.bfloat16), x_lo.astype(jnp.bfloat16)], axis=1)
        gu = jnp.dot(xs, wgu_bf[...], preferred_element_type=jnp.float32) + bgu_ref[...]
        gate = jnp.minimum(gu[:, :d_ff], SWIGLU_LIMIT)
        up = jnp.clip(gu[:, d_ff:], -SWIGLU_LIMIT, SWIGLU_LIMIT)
        act = gate * _sigmoid(SWIGLU_ALPHA * gate) * (up + 1.0)
        y = jnp.dot(act.astype(jnp.bfloat16), wdn_bf[...],
                    preferred_element_type=jnp.float32) + bdn_ref[...]
        _store_rows(y_ref, _pack_rows(y))


def _experts(items, xs, w_gu, b_gu, w_dn, b_dn, layer):
    n_rows = xs.shape[0] // PIECES
    half = PIECES * V7X_LANES
    d = 2 * half
    blk, exp, n_valid = items
    n_items = blk.shape[0]
    rows = EXPERT_BLOCK
    d_ff2 = w_gu.shape[-1]
    d_ff = w_dn.shape[-2]
    n_exp = w_gu.shape[1]
    depth = w_gu.shape[0]
    wmap = lambda i, blk, exp, nv: (layer, exp[i], 0, 0)
    grid_spec = pltpu.PrefetchScalarGridSpec(
        num_scalar_prefetch=3,
        grid=(n_items,),
        in_specs=[
            pl.BlockSpec((rows * PIECES, V7X_LANES), lambda i, blk, exp, nv: (blk[i], 0)),
            pl.BlockSpec((None, None, d, d_ff2), wmap),
            pl.BlockSpec((None, None, 1, d_ff2), wmap),
            pl.BlockSpec((None, None, d_ff, d), wmap),
            pl.BlockSpec((None, None, 1, d), wmap),
        ],
        out_specs=pl.BlockSpec((rows * PIECES, V7X_LANES), lambda i, blk, exp, nv: (blk[i], 0)),
        scratch_shapes=[
            pltpu.VMEM((d, d_ff2), jnp.bfloat16),
            pltpu.VMEM((d_ff, d), jnp.bfloat16),
        ],
    )
    return pl.pallas_call(
        _experts_kernel,
        grid_spec=grid_spec,
        out_shape=jax.ShapeDtypeStruct((n_rows * PIECES, V7X_LANES), jnp.uint32),
        compiler_params=pltpu.CompilerParams(
            dimension_semantics=("arbitrary",),
            vmem_limit_bytes=V7X_VMEM_LIMIT_BYTES),
        name="experts",
    )(blk, exp, n_valid, xs, w_gu, b_gu.reshape(depth, n_exp, 1, d_ff2), w_dn,
      b_dn.reshape(depth, n_exp, 1, d))


def _expert_items(counts, capacity, n_assign):
    n_exp = counts.shape[0]
    rows = EXPERT_BLOCK
    n_items = n_assign // rows + n_exp
    per_exp = (counts + rows - 1) // rows
    item_end = jnp.cumsum(per_exp)
    item_start = item_end - per_exp
    total = item_end[-1]
    idx = jnp.arange(n_items, dtype=jnp.int32)
    pick = jnp.minimum(idx, total - 1)
    exp = jnp.sum((item_end[None, :] <= pick[:, None]).astype(jnp.int32), axis=1)
    hot = exp[:, None] == jnp.arange(n_exp, dtype=jnp.int32)[None, :]
    start_e = jnp.sum(jnp.where(hot, item_start[None, :], 0), axis=1)
    count_e = jnp.sum(jnp.where(hot, counts[None, :], 0), axis=1)
    sub = pick - start_e
    blk = exp * (capacity // rows) + sub
    n_valid = jnp.where(idx < total, jnp.minimum(count_e - sub * rows, rows), 0)
    return blk.astype(jnp.int32), exp.astype(jnp.int32), n_valid.astype(jnp.int32)


def _combine_kernel(dest_cur, dest_nxt, x1_ref, w_ref, mod_ref, fin_ref, y_ref, o_ref,
                    gbuf, wt_v, sem):
    tmd, d = x1_ref.shape
    half = d // 2
    step = pl.program_id(0) * pl.num_programs(1) + pl.program_id(1)
    n_steps = pl.num_programs(0) * pl.num_programs(1)
    slot = step % 2

    def gather_copy(which, t, k, row):
        return pltpu.make_async_copy(y_ref.at[pl.ds(row, PIECES)],
                                     gbuf.at[which, k, pl.ds(t * PIECES, PIECES)], sem.at[which])

    def tile_gather_wait(which):
        for k in range(TOP_K):
            pltpu.make_async_copy(y_ref.at[pl.ds(0, tmd * PIECES)], gbuf.at[which, k],
                                  sem.at[which]).wait()

    @pl.when(step == 0)
    def _():
        wt_v[...] = jnp.zeros_like(wt_v)

        def first_gather(t, carry):
            for k in range(TOP_K):
                gather_copy(0, t, k, dest_cur[k * tmd + t]).start(priority=k % 2)
            return carry

        lax.fori_loop(0, tmd, first_gather, 0)

    for t in range(tmd):
        for k in range(TOP_K):
            gather_copy(1 - slot, t, k, dest_nxt[k * tmd + t]).start(priority=k % 2)

    tile_gather_wait(slot)
    wt_v[0:TOP_K, :] = w_ref[...]
    w = wt_v[...].T
    acc_hi, acc_lo = None, None
    for k in range(TOP_K):
        y_hi, y_lo = _unpack_rows(_load_rows(gbuf.at[slot, k], tmd))
        wk = w[:, k:k + 1]
        acc_hi = wk * y_hi if acc_hi is None else acc_hi + wk * y_hi
        acc_lo = wk * y_lo if acc_lo is None else acc_lo + wk * y_lo
    g_f = mod_ref[:, 5 * d:6 * d]
    x_hi = x1_ref[:, 0:half] + g_f[:, 0:half] * acc_hi
    x_lo = x1_ref[:, half:d] + g_f[:, half:d] * acc_lo
    ms = (jnp.sum(x_hi * x_hi, axis=-1, keepdims=True)
          + jnp.sum(x_lo * x_lo, axis=-1, keepdims=True)) * (1.0 / d)
    r = lax.rsqrt(ms + NORM_EPS)
    o_ref[:, 0:half] = x_hi * r * fin_ref[:, 0:half]
    o_ref[:, half:d] = x_lo * r * fin_ref[:, half:d]

    @pl.when(step == n_steps - 1)
    def _():
        tile_gather_wait(1 - slot)


def _combine(dest, x1, top_w, mod_l, fin_g, y):
    bsz, seq, d = x1.shape
    tmd = COMBINE_TILE
    tiles = seq // tmd
    n_steps = bsz * tiles
    dest_flat = _tile_major(dest, tmd)
    return pl.pallas_call(
        _combine_kernel,
        grid=(bsz, tiles),
        in_specs=[
            pl.BlockSpec((tmd * TOP_K,), lambda b, s: (b * tiles + s,), memory_space=pltpu.SMEM),
            pl.BlockSpec((tmd * TOP_K,),
                         lambda b, s: (jnp.minimum(b * tiles + s + 1, n_steps - 1),),
                         memory_space=pltpu.SMEM),
            pl.BlockSpec((None, tmd, d), lambda b, s: (b, s, 0)),
            pl.BlockSpec((TOP_K, tmd), lambda b, s: (0, b * tiles + s)),
            pl.BlockSpec((None, 1, 6 * d), lambda b, s: (b, 0, 0)),
            pl.BlockSpec((1, d), lambda b, s: (0, 0)),
            pl.BlockSpec(memory_space=pl.ANY),
        ],
        out_specs=pl.BlockSpec((None, tmd, d), lambda b, s: (b, s, 0)),
        out_shape=jax.ShapeDtypeStruct((bsz, seq, d), jnp.float32),
        scratch_shapes=[
            pltpu.VMEM((2, TOP_K, tmd * PIECES, V7X_LANES), jnp.uint32),
            pltpu.VMEM((V7X_LANES, tmd), jnp.float32),
            pltpu.SemaphoreType.DMA((2,)),
        ],
        compiler_params=pltpu.CompilerParams(dimension_semantics=("arbitrary", "arbitrary")),
        name="final_combine",
    )(dest_flat, dest_flat, x1, top_w, mod_l, fin_g, y)


def kernel(x, c, w_ada, b_ada, norm_mix_g, w_in, conv_w, w_spatial, b_spatial, ln_v_g, w_out,
           norm_ffn_g, router_w, router_b, w_gate_up, b_gate_up, w_down, b_down, final_norm_g):
    bsz, seq, d = x.shape
    depth = w_ada.shape[0]
    n_exp = router_w.shape[-1]
    n_tok = bsz * seq
    assert seq % MIX_TILE == 0 and MIX_TILE % CHUNK == 0
    assert seq % COMBINE_TILE == 0 and n_tok % EXPERT_BLOCK == 0
    assert d // 2 == PIECES * V7X_LANES

    mod = _modulation(c, w_ada, b_ada)
    win_bf = w_in.astype(jnp.bfloat16)
    wout_bf = w_out.astype(jnp.bfloat16)
    bsp_full = jnp.repeat(jnp.swapaxes(b_spatial, 1, 2), d // GROUPS, axis=2)
    rwt = jnp.swapaxes(router_w, 1, 2)
    fin_g = final_norm_g.reshape(1, d)

    prev = None
    for l in range(depth):
        x, dest, top_w, cnt, xs = _mixer(
            x, mod[l], norm_mix_g[l].reshape(1, d), win_bf, conv_w[l], w_spatial[l],
            bsp_full[l], ln_v_g[l].reshape(1, d), wout_bf, norm_ffn_g[l].reshape(1, d),
            rwt[l], router_b[l].reshape(n_exp, 1), prev, l)
        items = _expert_items(cnt[:, 0], n_tok, n_tok * TOP_K)
        y = _experts(items, xs, w_gate_up, b_gate_up, w_down, b_down, l)
        prev = (dest, top_w, mod[l], y)
    return _combine(dest, x, top_w, mod[depth - 1], fin_g, y)
```
